```python
import math
import jax, jax.numpy as jnp
from jax import lax
import numpy as np

D_MODEL = 4096
BATCH = 2
SEQ = 8192
DEPTH = 4

POOL_WIDTH = D_MODEL // 2
POOL_WINDOWS = (2, 4, 8, 16)
N_POOL_GROUPS = len(POOL_WINDOWS)
POOL_GROUP = POOL_WIDTH // N_POOL_GROUPS
DIFF_HEAD_DIM = 64
DIFF_WIDTH = D_MODEL - POOL_WIDTH
DIFF_HEADS = DIFF_WIDTH // (2 * DIFF_HEAD_DIM)
EVEN_IN = POOL_WIDTH + 3 * DIFF_WIDTH
EVEN_OUT = POOL_WIDTH + DIFF_WIDTH
SWA_HEADS = 64
SWA_KV_HEADS = 8
SWA_HEAD_DIM = D_MODEL // SWA_HEADS
SWA_GROUP = SWA_HEADS // SWA_KV_HEADS
WINDOW = 128
SWA_Q_WIDTH = SWA_HEADS * SWA_HEAD_DIM
SWA_KV_WIDTH = SWA_KV_HEADS * SWA_HEAD_DIM
ODD_IN = SWA_Q_WIDTH + 2 * SWA_KV_WIDTH
D_FF = 4 * D_MODEL
Q_BLOCK = 128
RMS_EPS = 1e-5
N_EVEN = (DEPTH + 1) // 2
N_ODD = DEPTH // 2

kernel_name = 'hybrid_pool_diffattn_swa_sink_sqrelu'


def rmsnorm(x, g):
    xf = x.astype(jnp.float32)
    y = xf * lax.rsqrt(jnp.mean(xf * xf, axis=-1, keepdims=True) + RMS_EPS)
    return (y * g.astype(jnp.float32)).astype(x.dtype)


def lambda_init_fn(layer_idx):
    return 0.8 - 0.6 * math.exp(-0.3 * layer_idx)


def multiscale_pool(u, w_pool, pool_scale):
    b, s, _ = u.shape
    uf = u.astype(jnp.float32)
    cs = jnp.concatenate([jnp.zeros((b, 1, POOL_WIDTH), jnp.float32),
                          jnp.cumsum(uf, axis=1)], axis=1)
    t = jnp.arange(s)
    outs = []
    for g, w in enumerate(POOL_WINDOWS):
        sl = slice(g * POOL_GROUP, (g + 1) * POOL_GROUP)
        start = jnp.maximum(t + 1 - w, 0)
        win_sum = cs[:, 1:, sl] - cs[:, start, sl]
        count = jnp.minimum(t + 1, w).astype(jnp.float32)[None, :, None]
        outs.append(win_sum / count - uf[:, :, sl])
    p = jnp.stack(outs, axis=2).astype(u.dtype)
    y = jnp.einsum('bsgc,gcd->bsgd', p, w_pool)
    return y.reshape(b, s, POOL_WIDTH) * pool_scale


def diff_attention(q, k, v, lam, subln_g, lambda_init):
    b, s = q.shape[:2]
    nb = s // Q_BLOCK
    scale = DIFF_HEAD_DIM ** -0.5
    q_blocks = q.reshape(b, nb, Q_BLOCK, DIFF_HEADS, 2, DIFF_HEAD_DIM).transpose(1, 0, 2, 3, 4, 5)
    k_pos = jnp.arange(s)

    def block(args):
        qb, n = args
        q_pos = n * Q_BLOCK + jnp.arange(Q_BLOCK)
        sc = jnp.einsum('bqhcd,bkhcd->bhcqk', qb, k).astype(jnp.float32) * scale
        causal = k_pos[None, :] <= q_pos[:, None]
        sc = jnp.where(causal, sc, -jnp.inf)
        a = jax.nn.softmax(sc, axis=-1)
        a = a[:, :, 0] - lam * a[:, :, 1]
        return jnp.einsum('bhqk,bkhe->bqhe', a.astype(v.dtype), v)

    o = lax.map(block, (q_blocks, jnp.arange(nb)))
    o = o.transpose(1, 0, 2, 3, 4).reshape(b, s, DIFF_HEADS, 2 * DIFF_HEAD_DIM)
    o = rmsnorm(o, subln_g) * (1.0 - lambda_init)
    return o.reshape(b, s, DIFF_WIDTH)


def swa_sink_attention(q, k, v, sinks):
    b, s = q.shape[:2]
    nb = s // WINDOW
    qb = q.reshape(b, nb, WINDOW, SWA_KV_HEADS, SWA_GROUP, SWA_HEAD_DIM).transpose(1, 0, 2, 3, 4, 5)

    def banded(t):
        t = t.reshape(b, nb, WINDOW, SWA_KV_HEADS, SWA_HEAD_DIM)
        prev = jnp.pad(t[:, :-1], ((0, 0), (1, 0), (0, 0), (0, 0), (0, 0)))
        return jnp.concatenate([prev, t], axis=2).transpose(1, 0, 2, 3, 4)

    kk, vv = banded(k), banded(v)
    q_rel = WINDOW + jnp.arange(WINDOW)[:, None]
    k_rel = jnp.arange(2 * WINDOW)[None, :]
    band = (k_rel <= q_rel) & (q_rel - k_rel < WINDOW)
    sink_logit = sinks.astype(jnp.float32).reshape(SWA_KV_HEADS, SWA_GROUP)[None, :, :, None, None]
    scale = SWA_HEAD_DIM ** -0.5

    def block(args):
        qi, ki, vi, n = args
        sc = jnp.einsum('bqhgd,bkhd->bhgqk', qi, ki).astype(jnp.float32) * scale
        mask = band & ((n > 0) | (k_rel >= WINDOW))
        sc = jnp.where(mask, sc, -jnp.inf)
        logits = jnp.concatenate([sc, jnp.broadcast_to(sink_logit, sc.shape[:-1] + (1,))], axis=-1)
        p = jax.nn.softmax(logits, axis=-1)[..., :-1]
        return jnp.einsum('bhgqk,bkhd->bqhgd', p.astype(vi.dtype), vi)

    o = lax.map(block, (qb, kk, vv, jnp.arange(nb)))
    return o.transpose(1, 0, 2, 3, 4, 5).reshape(b, s, SWA_Q_WIDTH)


def squared_relu_mlp(h, w_up, w_down):
    a = jax.nn.relu(h @ w_up)
    return (a * a) @ w_down


def setup_inputs(seed: int = 0) -> dict:
    key = jax.random.key(seed)
    ks = jax.random.split(key, 24)
    f = jnp.float32

    def nrm(k, shape, scale):
        return jax.random.normal(k, shape, f) * scale

    return {
        'x': nrm(ks[0], (BATCH, SEQ, D_MODEL), 1.0),
        'norm_mix': 1.0 + nrm(ks[1], (DEPTH, D_MODEL), 0.05),
        'norm_mlp': 1.0 + nrm(ks[2], (DEPTH, D_MODEL), 0.05),
        'norm_final': 1.0 + nrm(ks[3], (D_MODEL,), 0.05),
        'w_in_even': nrm(ks[4], (N_EVEN, D_MODEL, EVEN_IN), D_MODEL ** -0.5),
        'w_pool': nrm(ks[5], (N_EVEN, N_POOL_GROUPS, POOL_GROUP, POOL_GROUP), POOL_GROUP ** -0.5),
        'pool_scale': 1.0 + nrm(ks[6], (N_EVEN, POOL_WIDTH), 0.1),
        'lambda_q1': nrm(ks[7], (N_EVEN, DIFF_HEAD_DIM), 0.1),
        'lambda_k1': nrm(ks[8], (N_EVEN, DIFF_HEAD_DIM), 0.1),
        'lambda_q2': nrm(ks[9], (N_EVEN, DIFF_HEAD_DIM), 0.1),
        'lambda_k2': nrm(ks[10], (N_EVEN, DIFF_HEAD_DIM), 0.1),
        'subln_g': 1.0 + nrm(ks[11], (N_EVEN, 2 * DIFF_HEAD_DIM), 0.05),
        'w_out_even': nrm(ks[12], (N_EVEN, EVEN_OUT, D_MODEL), EVEN_OUT ** -0.5),
        'w_in_odd': nrm(ks[13], (N_ODD, D_MODEL, ODD_IN), D_MODEL ** -0.5),
        'b_in_odd': nrm(ks[14], (N_ODD, ODD_IN), 0.02),
        'sinks': nrm(ks[15], (N_ODD, SWA_HEADS), 0.5),
        'w_out_odd': nrm(ks[16], (N_ODD, SWA_Q_WIDTH, D_MODEL), SWA_Q_WIDTH ** -0.5),
        'b_out_odd': nrm(ks[17], (N_ODD, D_MODEL), 0.02),
        'w_up': nrm(ks[18], (DEPTH, D_MODEL, D_FF), D_MODEL ** -0.5),
        'w_down': nrm(ks[19], (DEPTH, D_FF, D_MODEL), D_FF ** -0.5),
    }


def reference(x, norm_mix, norm_mlp, norm_final, w_in_even, w_pool, pool_scale,
              lambda_q1, lambda_k1, lambda_q2, lambda_k2, subln_g, w_out_even,
              w_in_odd, b_in_odd, sinks, w_out_odd, b_out_odd, w_up, w_down):
    b, s, _ = x.shape
    for i in range(DEPTH):
        h = rmsnorm(x, norm_mix[i])
        if i % 2 == 0:
            j = i // 2
            z = h @ w_in_even[j]
            u = z[..., :POOL_WIDTH]
            q = z[..., POOL_WIDTH:POOL_WIDTH + DIFF_WIDTH].reshape(b, s, DIFF_HEADS, 2, DIFF_HEAD_DIM)
            k = z[..., POOL_WIDTH + DIFF_WIDTH:POOL_WIDTH + 2 * DIFF_WIDTH].reshape(b, s, DIFF_HEADS, 2, DIFF_HEAD_DIM)
            v = z[..., POOL_WIDTH + 2 * DIFF_WIDTH:].reshape(b, s, DIFF_HEADS, 2 * DIFF_HEAD_DIM)
            lam_init = lambda_init_fn(i)
            lam = (jnp.exp(jnp.sum(lambda_q1[j] * lambda_k1[j]).astype(jnp.float32))
                   - jnp.exp(jnp.sum(lambda_q2[j] * lambda_k2[j]).astype(jnp.float32)) + lam_init)
            o_a = multiscale_pool(u, w_pool[j], pool_scale[j])
            o_b = diff_attention(q, k, v, lam, subln_g[j], lam_init)
            x = x + jnp.concatenate([o_a, o_b], axis=-1) @ w_out_even[j]
        else:
            j = i // 2
            z = h @ w_in_odd[j] + b_in_odd[j]
            q = z[..., :SWA_Q_WIDTH].reshape(b, s, SWA_KV_HEADS, SWA_GROUP, SWA_HEAD_DIM)
            k = z[..., SWA_Q_WIDTH:SWA_Q_WIDTH + SWA_KV_WIDTH].reshape(b, s, SWA_KV_HEADS, SWA_HEAD_DIM)
            v = z[..., SWA_Q_WIDTH + SWA_KV_WIDTH:].reshape(b, s, SWA_KV_HEADS, SWA_HEAD_DIM)
            o_c = swa_sink_attention(q, k, v, sinks[j])
            x = x + o_c @ w_out_odd[j] + b_out_odd[j]
        h = rmsnorm(x, norm_mlp[i])
        x = x + squared_relu_mlp(h, w_up[i], w_down[i])
    return rmsnorm(x, norm_final)
```

```python
import functools
import math

import jax
import jax.numpy as jnp
from jax import lax
from jax.experimental import pallas as pl
from jax.experimental.pallas import tpu as pltpu

D_MODEL = 4096
DEPTH = 4
POOL_WIDTH = D_MODEL // 2
POOL_WINDOWS = (2, 4, 8, 16)
POOL_GROUP = POOL_WIDTH // len(POOL_WINDOWS)
MAX_WINDOW = max(POOL_WINDOWS)
DIFF_HEAD_DIM = 64
DIFF_WIDTH = D_MODEL - POOL_WIDTH
DIFF_HEADS = DIFF_WIDTH // (2 * DIFF_HEAD_DIM)
SWA_HEADS = 64
SWA_KV_HEADS = 8
SWA_HEAD_DIM = D_MODEL // SWA_HEADS
SWA_GROUP = SWA_HEADS // SWA_KV_HEADS
SWA_WINDOW = 128
SWA_Q_WIDTH = SWA_HEADS * SWA_HEAD_DIM
SWA_KV_WIDTH = SWA_KV_HEADS * SWA_HEAD_DIM
D_FF = 4 * D_MODEL
RMS_EPS = 1e-5

V7X_LANES = 128
V7X_VMEM_BYTES = 64 * 1024 * 1024
VMEM_LIMIT_BYTES = V7X_VMEM_BYTES - 8 * 1024 * 1024

BF16 = jnp.bfloat16
F32 = jnp.float32
NEG_INF = float("-inf")


def _compiler_params(semantics):
    return pltpu.CompilerParams(dimension_semantics=semantics,
                                vmem_limit_bytes=VMEM_LIMIT_BYTES)


def _rmsnorm_kernel(x_ref, g_ref, o_ref):
    x = x_ref[...]
    ms = jnp.mean(x * x, axis=-1, keepdims=True)
    o_ref[...] = (x * lax.rsqrt(ms + RMS_EPS) * g_ref[...]).astype(o_ref.dtype)


def rmsnorm(x, g, out_dtype, block_rows=256):
    t, d = x.shape
    return pl.pallas_call(
        _rmsnorm_kernel,
        grid=(t // block_rows,),
        in_specs=[pl.BlockSpec((block_rows, d), lambda i: (i, 0)),
                  pl.BlockSpec((1, d), lambda i: (0, 0))],
        out_specs=pl.BlockSpec((block_rows, d), lambda i: (i, 0)),
        out_shape=jax.ShapeDtypeStruct((t, d), out_dtype),
        compiler_params=_compiler_params(("parallel",)),
        name="rmsnorm",
    )(x, g.reshape(1, d))


def _matmul_kernel(*refs, nk, relu2, has_bias, has_res):
    it = iter(refs)
    x_ref, w_ref = next(it), next(it)
    b_ref = next(it) if has_bias else None
    r_ref = next(it) if has_res else None
    o_ref = next(it)
    acc_ref = next(it) if nk > 1 else None

    part = jnp.dot(x_ref[...], w_ref[...], preferred_element_type=F32)

    def epilogue(acc):
        if has_bias:
            acc = acc + b_ref[...]
        if relu2:
            a = jnp.maximum(acc, 0.0)
            acc = a * a
        if has_res:
            acc = acc + r_ref[...]
        o_ref[...] = acc.astype(o_ref.dtype)

    if nk == 1:
        epilogue(part)
    else:
        k = pl.program_id(2)

        @pl.when(k == 0)
        def _():
            acc_ref[...] = part

        @pl.when(k > 0)
        def _():
            acc_ref[...] += part

        @pl.when(k == nk - 1)
        def _():
            epilogue(acc_ref[...])


def matmul(x, w, *, bias=None, residual=None, relu2=False, out_dtype=BF16,
           bm=1024, bn=1024, bk=None):
    m, kdim = x.shape
    n = w.shape[1]
    bk = kdim if bk is None else bk
    nk = kdim // bk
    in_specs = [pl.BlockSpec((bm, bk), lambda i, j, k: (i, k)),
                pl.BlockSpec((bk, bn), lambda i, j, k: (k, j))]
    args = [x, w]
    if bias is not None:
        in_specs.append(pl.BlockSpec((1, bn), lambda i, j, k: (0, j)))
        args.append(bias.reshape(1, n).astype(F32))
    if residual is not None:
        in_specs.append(pl.BlockSpec((bm, bn), lambda i, j, k: (i, j)))
        args.append(residual)
    kern = functools.partial(_matmul_kernel, nk=nk, relu2=relu2,
                             has_bias=bias is not None, has_res=residual is not None)
    return pl.pallas_call(
        kern,
        grid=(m // bm, n // bn, nk),
        in_specs=in_specs,
        out_specs=pl.BlockSpec((bm, bn), lambda i, j, k: (i, j)),
        out_shape=jax.ShapeDtypeStruct((m, n), out_dtype),
        scratch_shapes=[pltpu.VMEM((bm, bn), F32)] if nk > 1 else [],
        compiler_params=_compiler_params(("parallel", "parallel", "arbitrary")),
        name="matmul",
    )(*args)


def _pool_kernel(u_ref, halo_ref, w_ref, s_ref, o_ref, *, rows, blocks_per_seq):
    i = pl.program_id(0)
    seq_blk = i % blocks_per_seq
    halo = jnp.where(seq_blk == 0, 0.0, halo_ref[...])
    t = seq_blk * rows + lax.broadcasted_iota(jnp.int32, (rows, 1), 0)
    for g, win in enumerate(POOL_WINDOWS):
        lo, hi = g * POOL_GROUP, (g + 1) * POOL_GROUP
        cur = u_ref[:, lo:hi]
        ext = jnp.concatenate([halo[:, lo:hi], cur], axis=0)
        win_sum = cur
        for j in range(1, win):
            win_sum = win_sum + ext[MAX_WINDOW - j:MAX_WINDOW - j + rows]
        count = jnp.minimum(t + 1, win).astype(F32)
        p = (win_sum / count - cur).astype(BF16)
        y = jnp.dot(p, w_ref[g], preferred_element_type=F32)
        o_ref[:, lo:hi] = (y * s_ref[:, lo:hi]).astype(o_ref.dtype)


def multiscale_pool(u, w_pool, pool_scale, seq_len, rows=512):
    t, c = u.shape
    halo_per_blk = rows // MAX_WINDOW
    kern = functools.partial(_pool_kernel, rows=rows, blocks_per_seq=seq_len // rows)
    return pl.pallas_call(
        kern,
        grid=(t // rows,),
        in_specs=[pl.BlockSpec((rows, c), lambda i: (i, 0)),
                  pl.BlockSpec((MAX_WINDOW, c),
                               lambda i: (jnp.maximum(i * halo_per_blk - 1, 0), 0)),
                  pl.BlockSpec((len(POOL_WINDOWS), POOL_GROUP, POOL_GROUP),
                               lambda i: (0, 0, 0)),
                  pl.BlockSpec((1, c), lambda i: (0, 0))],
        out_specs=pl.BlockSpec((rows, c), lambda i: (i, 0)),
        out_shape=jax.ShapeDtypeStruct((t, c), BF16),
        compiler_params=_compiler_params(("parallel",)),
        name="multiscale_pool",
    )(u, u, w_pool, pool_scale.reshape(1, c))


def _diff_attn_kernel(q_ref, k_ref, v_ref, lq1_ref, lk1_ref, lq2_ref, lk2_ref, g_ref,
                      o_ref, *, blk, lam_init):
    i = pl.program_id(2)
    d = DIFF_HEAD_DIM
    q = q_ref[...] * jnp.asarray(d ** -0.5, q_ref.dtype)
    lane = lax.broadcasted_iota(jnp.int32, q.shape, 1)
    zero = jnp.zeros_like(q)
    q_maps = (jnp.where(lane < d, q, zero), jnp.where(lane >= d, q, zero))

    def step(j, carry, diagonal):
        start = pl.multiple_of(j * blk, blk)
        kb = k_ref[pl.ds(start, blk), :]
        vb = v_ref[pl.ds(start, blk), :]
        out = []
        for c in range(2):
            m_prev, l_prev, acc_prev = carry[c]
            s = lax.dot_general(q_maps[c], kb, (((1,), (1,)), ((), ())),
                                preferred_element_type=F32)
            if diagonal:
                row = lax.broadcasted_iota(jnp.int32, s.shape, 0)
                col = lax.broadcasted_iota(jnp.int32, s.shape, 1)
                s = jnp.where(col <= row, s, NEG_INF)
            m_new = jnp.maximum(m_prev, jnp.max(s, axis=-1, keepdims=True))
            alpha = jnp.exp(m_prev - m_new)
            p = jnp.exp(s - m_new)
            l_new = alpha * l_prev + jnp.sum(p, axis=-1, keepdims=True)
            acc_new = alpha * acc_prev + jnp.dot(p.astype(BF16), vb,
                                                 preferred_element_type=F32)
            out.append((m_new, l_new, acc_new))
        return tuple(out)

    init_one = (jnp.full((blk, 1), NEG_INF, F32), jnp.zeros((blk, 1), F32),
                jnp.zeros((blk, 2 * d), F32))
    carry = lax.fori_loop(0, i, lambda j, c: step(j, c, False), (init_one, init_one))
    (_, l1, acc1), (_, l2, acc2) = step(i, carry, True)

    lam = (jnp.exp(jnp.sum(lq1_ref[...] * lk1_ref[...], axis=-1, keepdims=True))
           - jnp.exp(jnp.sum(lq2_ref[...] * lk2_ref[...], axis=-1, keepdims=True))
           + lam_init)
    o = acc1 / l1 - lam * (acc2 / l2)
    ms = jnp.mean(o * o, axis=-1, keepdims=True)
    o = o * lax.rsqrt(ms + RMS_EPS) * g_ref[...] * (1.0 - lam_init)
    o_ref[...] = o.astype(o_ref.dtype)


def diff_attention(qkv, lq1, lk1, lq2, lk2, subln_g, lam_init, batch, seq_len, blk=512):
    t = qkv.shape[0]
    nq = seq_len // blk
    hw = 2 * DIFF_HEAD_DIM
    vec = lambda a: a.reshape(1, -1).astype(F32)
    vec_spec = lambda n: pl.BlockSpec((1, n), lambda b, h, i: (0, 0))
    kern = functools.partial(_diff_attn_kernel, blk=blk, lam_init=lam_init)
    return pl.pallas_call(
        kern,
        grid=(batch, DIFF_HEADS, nq),
        in_specs=[pl.BlockSpec((blk, hw), lambda b, h, i: (b * nq + i, h)),
                  pl.BlockSpec((seq_len, hw), lambda b, h, i: (b, DIFF_HEADS + h)),
                  pl.BlockSpec((seq_len, hw), lambda b, h, i: (b, 2 * DIFF_HEADS + h)),
                  vec_spec(DIFF_HEAD_DIM), vec_spec(DIFF_HEAD_DIM),
                  vec_spec(DIFF_HEAD_DIM), vec_spec(DIFF_HEAD_DIM), vec_spec(hw)],
        out_specs=pl.BlockSpec((blk, hw), lambda b, h, i: (b * nq + i, h)),
        out_shape=jax.ShapeDtypeStruct((t, DIFF_WIDTH), BF16),
        compiler_params=_compiler_params(("parallel", "parallel", "arbitrary")),
        name="diff_attention",
    )(qkv, qkv, qkv, vec(lq1), vec(lk1), vec(lq2), vec(lk2), vec(subln_g))


KV_PAIR = V7X_LANES // SWA_HEAD_DIM
Q_PAIR_WIDTH = KV_PAIR * SWA_GROUP * SWA_HEAD_DIM


def _swap_halves(a):
    h = a.shape[-1] // 2
    return jnp.concatenate([a[:, h:], a[:, :h]], axis=-1)


def _swa_kernel(sink_ref, q_ref, k_ref, v_ref, o_ref, *, rows):
    p_idx = pl.program_id(1)
    i = pl.program_id(2)
    w, d = SWA_WINDOW, SWA_HEAD_DIM
    lane = lax.broadcasted_iota(jnp.int32, (w, 2 * d), 1)
    kv_lane = lax.broadcasted_iota(jnp.int32, (2 * w, 2 * d), 1)
    scale = jnp.asarray(d ** -0.5, BF16)
    for r in range(rows // w):
        base = i * rows + r * w
        start = pl.multiple_of(jnp.maximum(base - w, 0), w)
        q_pos = base + lax.broadcasted_iota(jnp.int32, (w, 2 * w), 0)
        k_pos = start + lax.broadcasted_iota(jnp.int32, (w, 2 * w), 1)
        mask = (k_pos <= q_pos) & (q_pos - k_pos < w)
        kk = k_ref[pl.ds(start, 2 * w), :]
        vv = v_ref[pl.ds(start, 2 * w), :]
        kk_sw, vv_sw = _swap_halves(kk), _swap_halves(vv)
        for hh in range(KV_PAIR):
            k_lo, k_hi = (kk, kk_sw) if hh == 0 else (kk_sw, kk)
            v_lo, v_hi = (vv, vv_sw) if hh == 0 else (vv_sw, vv)
            v_lo = jnp.where(kv_lane < d, v_lo, jnp.zeros_like(v_lo))
            v_hi = jnp.where(kv_lane >= d, v_hi, jnp.zeros_like(v_hi))
            for gp in range(SWA_GROUP // 2):
                col = (hh * SWA_GROUP + 2 * gp) * d
                qq = q_ref[r * w:(r + 1) * w, col:col + 2 * d] * scale
                zero = jnp.zeros_like(qq)
                probs = []
                for half, (qh, kh) in enumerate(((jnp.where(lane < d, qq, zero), k_lo),
                                                 (jnp.where(lane >= d, qq, zero), k_hi))):
                    s = lax.dot_general(qh, kh, (((1,), (1,)), ((), ())),
                                        preferred_element_type=F32)
                    s = jnp.where(mask, s, NEG_INF)
                    head = (p_idx * KV_PAIR + hh) * SWA_GROUP + 2 * gp + half
                    sink = sink_ref[head]
                    m = jnp.maximum(jnp.max(s, axis=-1, keepdims=True), sink)
                    e = jnp.exp(s - m)
                    denom = jnp.sum(e, axis=-1, keepdims=True) + jnp.exp(sink - m)
                    probs.append((e / denom).astype(BF16))
                o = (jnp.dot(probs[0], v_lo, preferred_element_type=F32)
                     + jnp.dot(probs[1], v_hi, preferred_element_type=F32))
                o_ref[r * w:(r + 1) * w, col:col + 2 * d] = o.astype(o_ref.dtype)


def swa_attention(qkv, sinks, batch, seq_len, rows=512):
    t = qkv.shape[0]
    nq = seq_len // rows
    n_pairs = SWA_KV_HEADS // KV_PAIR
    k_blk0 = SWA_Q_WIDTH // V7X_LANES
    v_blk0 = (SWA_Q_WIDTH + SWA_KV_WIDTH) // V7X_LANES
    kern = functools.partial(_swa_kernel, rows=rows)
    return pl.pallas_call(
        kern,
        grid=(batch, n_pairs, nq),
        in_specs=[pl.BlockSpec(memory_space=pltpu.SMEM),
                  pl.BlockSpec((rows, Q_PAIR_WIDTH), lambda b, p, i: (b * nq + i, p)),
                  pl.BlockSpec((seq_len, V7X_LANES), lambda b, p, i: (b, k_blk0 + p)),
                  pl.BlockSpec((seq_len, V7X_LANES), lambda b, p, i: (b, v_blk0 + p))],
        out_specs=pl.BlockSpec((rows, Q_PAIR_WIDTH), lambda b, p, i: (b * nq + i, p)),
        out_shape=jax.ShapeDtypeStruct((t, SWA_Q_WIDTH), BF16),
        compiler_params=_compiler_params(("parallel", "parallel", "arbitrary")),
        name="swa_attention",
    )(sinks.astype(F32), qkv, qkv, qkv)


def _lambda_init(layer_idx):
    return 0.8 - 0.6 * math.exp(-0.3 * layer_idx)


def kernel(x, norm_mix, norm_mlp, norm_final, w_in_even, w_pool, pool_scale, lambda_q1, lambda_k1, lambda_q2, lambda_k2, subln_g, w_out_even, w_in_odd, b_in_odd, sinks, w_out_odd, b_out_odd, w_up, w_down):
    b, s, d = x.shape
    x = x.reshape(b * s, d)
    for i in range(DEPTH):
        j = i // 2
        h = rmsnorm(x, norm_mix[i], BF16)
        if i % 2 == 0:
            w_in = w_in_even[j].astype(BF16)
            u = matmul(h, w_in[:, :POOL_WIDTH], out_dtype=F32)
            qkv = matmul(h, w_in[:, POOL_WIDTH:])
            o_a = multiscale_pool(u, w_pool[j].astype(BF16), pool_scale[j], s)
            o_b = diff_attention(qkv, lambda_q1[j], lambda_k1[j], lambda_q2[j], lambda_k2[j],
                                 subln_g[j], _lambda_init(i), b, s)
            o = jnp.concatenate([o_a, o_b], axis=-1)
            x = matmul(o, w_out_even[j].astype(BF16), residual=x, out_dtype=F32)
        else:
            qkv = matmul(h, w_in_odd[j].astype(BF16), bias=b_in_odd[j])
            o_c = swa_attention(qkv, sinks[j], b, s)
            x = matmul(o_c, w_out_odd[j].astype(BF16), bias=b_out_odd[j], residual=x,
                       out_dtype=F32)
        h = rmsnorm(x, norm_mlp[i], BF16)
        a = matmul(h, w_up[i].astype(BF16), relu2=True)
        x = matmul(a, w_down[i].astype(BF16), residual=x, out_dtype=F32, bk=2048)
    return rmsnorm(x, norm_final, F32).reshape(b, s, d)
```

```python
import functools
import math

import jax
import jax.numpy as jnp
from jax import lax
from jax.experimental import pallas as pl
from jax.experimental.pallas import tpu as pltpu

D_MODEL = 4096
DEPTH = 4
POOL_WIDTH = D_MODEL // 2
POOL_WINDOWS = (2, 4, 8, 16)
POOL_GROUP = POOL_WIDTH // len(POOL_WINDOWS)
MAX_WINDOW = max(POOL_WINDOWS)
DIFF_HEAD_DIM = 64
DIFF_WIDTH = D_MODEL - POOL_WIDTH
DIFF_HEADS = DIFF_WIDTH // (2 * DIFF_HEAD_DIM)
SWA_HEADS = 64
SWA_KV_HEADS = 8
SWA_HEAD_DIM = D_MODEL // SWA_HEADS
SWA_GROUP = SWA_HEADS // SWA_KV_HEADS
SWA_WINDOW = 128
SWA_Q_WIDTH = SWA_HEADS * SWA_HEAD_DIM
SWA_KV_WIDTH = SWA_KV_HEADS * SWA_HEAD_DIM
D_FF = 4 * D_MODEL
RMS_EPS = 1e-5

V7X_LANES = 128
V7X_VMEM_BYTES = 64 * 1024 * 1024
VMEM_LIMIT_BYTES = V7X_VMEM_BYTES - 8 * 1024 * 1024

V7X_MXU_COLS = 256
ACC_STRIP = V7X_MXU_COLS

BF16 = jnp.bfloat16
F32 = jnp.float32
NEG_INF = float("-inf")


def _compiler_params(semantics):
    return pltpu.CompilerParams(dimension_semantics=semantics,
                                vmem_limit_bytes=VMEM_LIMIT_BYTES)


def _rmsnorm_kernel(x_ref, g_ref, o_ref):
    x = x_ref[...]
    ms = jnp.mean(x * x, axis=-1, keepdims=True)
    o_ref[...] = (x * lax.rsqrt(ms + RMS_EPS) * g_ref[...]).astype(o_ref.dtype)


def rmsnorm(x, g, out_dtype, block_rows=256):
    t, d = x.shape
    return pl.pallas_call(
        _rmsnorm_kernel,
        grid=(t // block_rows,),
        in_specs=[pl.BlockSpec((block_rows, d), lambda i: (i, 0)),
                  pl.BlockSpec((1, d), lambda i: (0, 0))],
        out_specs=pl.BlockSpec((block_rows, d), lambda i: (i, 0)),
        out_shape=jax.ShapeDtypeStruct((t, d), out_dtype),
        compiler_params=_compiler_params(("parallel",)),
        name="rmsnorm",
    )(x, g.reshape(1, d))


def _matmul_kernel(*refs, nk, n_parts, relu2, has_bias, has_scale, has_res):
    it = iter(refs)
    x_refs = [next(it) for _ in range(n_parts)]
    w_ref = next(it)
    b_ref = next(it) if has_bias else None
    c_ref = next(it) if has_scale else None
    r_ref = next(it) if has_res else None
    o_ref = next(it)
    acc_ref = next(it) if nk > 1 else None

    def epilogue(acc):
        if has_bias:
            acc = acc + b_ref[...]
        if has_scale:
            acc = acc * c_ref[...]
        if relu2:
            a = jnp.maximum(acc, 0.0)
            acc = a * a
        if has_res:
            acc = acc + r_ref[...]
        o_ref[...] = acc.astype(o_ref.dtype)

    if nk == 1:
        kp = w_ref.shape[0] // n_parts
        acc = jnp.dot(x_refs[0][...], w_ref[0:kp, :], preferred_element_type=F32)
        for p in range(1, n_parts):
            acc += jnp.dot(x_refs[p][...], w_ref[p * kp:(p + 1) * kp, :],
                           preferred_element_type=F32)
        epilogue(acc)
        return

    k = pl.program_id(2)

    @pl.when(k == 0)
    def _():
        acc_ref[...] = jnp.zeros_like(acc_ref)

    strip = min(ACC_STRIP, acc_ref.shape[1])
    for c0 in range(0, acc_ref.shape[1], strip):
        acc_ref[:, c0:c0 + strip] += jnp.dot(x_refs[0][...], w_ref[:, c0:c0 + strip],
                                             preferred_element_type=F32)

    @pl.when(k == nk - 1)
    def _():
        epilogue(acc_ref[...])


def matmul(x_parts, w_stack, layer, *, col0=0, n=None, bias=None, col_scale=None,
           residual=None, relu2=False, out_dtype=BF16, bm=1024, bn=1024, bk=None):
    x_parts = list(x_parts)
    m = x_parts[0].shape[0]
    kdim = sum(xp.shape[1] for xp in x_parts)
    n = w_stack.shape[2] - col0 if n is None else n
    bm, bn = math.gcd(bm, m), math.gcd(math.gcd(bn, n), col0) if col0 else math.gcd(bn, n)
    bk = kdim if bk is None else math.gcd(bk, kdim)
    nk = kdim // bk
    assert nk == 1 or len(x_parts) == 1
    jb0 = col0 // bn
    in_specs = [pl.BlockSpec((bm, bk // len(x_parts)), lambda i, j, k: (i, k))
                for _ in x_parts]
    in_specs.append(pl.BlockSpec((None, bk, bn), lambda i, j, k: (layer, k, jb0 + j)))
    args = x_parts + [w_stack]
    if bias is not None:
        in_specs.append(pl.BlockSpec((1, bn), lambda i, j, k: (0, j)))
        args.append(bias.reshape(1, n).astype(F32))
    if col_scale is not None:
        in_specs.append(pl.BlockSpec((1, bn), lambda i, j, k: (0, j)))
        args.append(col_scale.reshape(1, n).astype(F32))
    if residual is not None:
        in_specs.append(pl.BlockSpec((bm, bn), lambda i, j, k: (i, j)))
        args.append(residual)
    kern = functools.partial(_matmul_kernel, nk=nk, n_parts=len(x_parts), relu2=relu2,
                             has_bias=bias is not None, has_scale=col_scale is not None,
                             has_res=residual is not None)
    return pl.pallas_call(
        kern,
        grid=(m // bm, n // bn, nk),
        in_specs=in_specs,
        out_specs=pl.BlockSpec((bm, bn), lambda i, j, k: (i, j)),
        out_shape=jax.ShapeDtypeStruct((m, n), out_dtype),
        scratch_shapes=[pltpu.VMEM((bm, bn), F32)] if nk > 1 else [],
        compiler_params=_compiler_params(("parallel", "parallel", "arbitrary")),
        name="matmul",
    )(*args)


def _pool_kernel(u_ref, halo_ref, w_ref, s_ref, o_ref, *, rows, blocks_per_seq):
    i = pl.program_id(0)
    seq_blk = i % blocks_per_seq
    halo = jnp.where(seq_blk == 0, 0.0, halo_ref[...])
    t = seq_blk * rows + lax.broadcasted_iota(jnp.int32, (rows, 1), 0)
    for g, win in enumerate(POOL_WINDOWS):
        lo, hi = g * POOL_GROUP, (g + 1) * POOL_GROUP
        cur = u_ref[:, lo:hi]
        ext = jnp.concatenate([halo[:, lo:hi], cur], axis=0)
        win_sum = cur
        for j in range(1, win):
            win_sum = win_sum + ext[MAX_WINDOW - j:MAX_WINDOW - j + rows]
        count = jnp.minimum(t + 1, win).astype(F32)
        p = (win_sum / count - cur).astype(BF16)
        y = jnp.dot(p, w_ref[g], preferred_element_type=F32)
        o_ref[:, lo:hi] = (y * s_ref[:, lo:hi]).astype(o_ref.dtype)


def multiscale_pool(u, w_pool, pool_scale, seq_len, rows=512):
    t, c = u.shape
    halo_per_blk = rows // MAX_WINDOW
    kern = functools.partial(_pool_kernel, rows=rows, blocks_per_seq=seq_len // rows)
    return pl.pallas_call(
        kern,
        grid=(t // rows,),
        in_specs=[pl.BlockSpec((rows, c), lambda i: (i, 0)),
                  pl.BlockSpec((MAX_WINDOW, c),
                               lambda i: (jnp.maximum(i * halo_per_blk - 1, 0), 0)),
                  pl.BlockSpec((len(POOL_WINDOWS), POOL_GROUP, POOL_GROUP),
                               lambda i: (0, 0, 0)),
                  pl.BlockSpec((1, c), lambda i: (0, 0))],
        out_specs=pl.BlockSpec((rows, c), lambda i: (i, 0)),
        out_shape=jax.ShapeDtypeStruct((t, c), BF16),
        compiler_params=_compiler_params(("parallel",)),
        name="multiscale_pool",
    )(u, u, w_pool, pool_scale.reshape(1, c))


def _diff_attn_kernel(q_ref, k_ref, v_ref, lq1_ref, lk1_ref, lq2_ref, lk2_ref, g_ref,
                      o_ref, sa_scr, sb_scr, *, blk, lam_init):
    i = pl.program_id(2)
    d = DIFF_HEAD_DIM
    qt = q_ref[...].astype(F32).T.astype(BF16)
    feat = lax.broadcasted_iota(jnp.int32, qt.shape, 0)
    zero = jnp.zeros_like(qt)
    qt_maps = (jnp.where(feat < d, qt, zero), jnp.where(feat >= d, qt, zero))

    def scores(j, s_scr):
        kb = k_ref[pl.ds(pl.multiple_of(j * blk, blk), blk), :]
        for c in range(2):
            s_scr[c] = jnp.dot(kb, qt_maps[c], preferred_element_type=F32)

    def update(j, s_scr, stats, diagonal):
        vb = v_ref[pl.ds(pl.multiple_of(j * blk, blk), blk), :]
        out = []
        for c in range(2):
            m_prev, l_prev, acc_prev = stats[c]
            s = s_scr[c]
            if diagonal:
                key = lax.broadcasted_iota(jnp.int32, s.shape, 0)
                qry = lax.broadcasted_iota(jnp.int32, s.shape, 1)
                s = jnp.where(key <= qry, s, NEG_INF)
            m_new = jnp.maximum(m_prev, jnp.max(s, axis=0, keepdims=True))
            alpha = jnp.exp2(m_prev - m_new)
            p = jnp.exp2(s - m_new)
            l_new = alpha * l_prev + jnp.sum(p, axis=0, keepdims=True)
            pv = lax.dot_general(vb, p.astype(BF16), (((0,), (0,)), ((), ())),
                                 preferred_element_type=F32)
            out.append((m_new, l_new, alpha * acc_prev + pv))
        return tuple(out)

    init_one = (jnp.full((1, blk), NEG_INF, F32), jnp.zeros((1, blk), F32),
                jnp.zeros((2 * d, blk), F32))

    scores(0, sa_scr)

    def pair(jj, stats):
        j = 2 * jj
        scores(j + 1, sb_scr)
        stats = update(j, sa_scr, stats, False)
        scores(j + 2, sa_scr)
        return update(j + 1, sb_scr, stats, False)

    stats = lax.fori_loop(0, i // 2, pair, (init_one, init_one))

    def odd_tail(stats):
        scores(i, sb_scr)
        stats = update(i - 1, sa_scr, stats, False)
        return update(i, sb_scr, stats, True)

    def even_tail(stats):
        return update(i, sa_scr, stats, True)

    (_, l1, acc1), (_, l2, acc2) = lax.cond(i % 2 == 1, odd_tail, even_tail, stats)

    lam = (jnp.exp(jnp.sum(lq1_ref[...] * lk1_ref[...], axis=-1, keepdims=True))
           - jnp.exp(jnp.sum(lq2_ref[...] * lk2_ref[...], axis=-1, keepdims=True))
           + lam_init)
    o = (acc1 / l1 - lam * (acc2 / l2)).T
    ms = jnp.mean(o * o, axis=-1, keepdims=True)
    o = o * lax.rsqrt(ms + RMS_EPS) * g_ref[...] * (1.0 - lam_init)
    o_ref[...] = o.astype(o_ref.dtype)


def diff_attention(qkv, lq1, lk1, lq2, lk2, subln_g, lam_init, batch, seq_len, blk=512):
    t = qkv.shape[0]
    nq = seq_len // blk
    hw = 2 * DIFF_HEAD_DIM
    vec = lambda a: a.reshape(1, -1).astype(F32)
    vec_spec = lambda n: pl.BlockSpec((1, n), lambda b, h, i: (0, 0))
    kern = functools.partial(_diff_attn_kernel, blk=blk, lam_init=lam_init)
    return pl.pallas_call(
        kern,
        grid=(batch, DIFF_HEADS, nq),
        in_specs=[pl.BlockSpec((blk, hw), lambda b, h, i: (b * nq + i, h)),
                  pl.BlockSpec((seq_len, hw), lambda b, h, i: (b, DIFF_HEADS + h)),
                  pl.BlockSpec((seq_len, hw), lambda b, h, i: (b, 2 * DIFF_HEADS + h)),
                  vec_spec(DIFF_HEAD_DIM), vec_spec(DIFF_HEAD_DIM),
                  vec_spec(DIFF_HEAD_DIM), vec_spec(DIFF_HEAD_DIM), vec_spec(hw)],
        out_specs=pl.BlockSpec((blk, hw), lambda b, h, i: (b * nq + i, h)),
        out_shape=jax.ShapeDtypeStruct((t, DIFF_WIDTH), BF16),
        scratch_shapes=[pltpu.VMEM((2, blk, blk), F32)] * 2,
        compiler_params=_compiler_params(("parallel", "parallel", "arbitrary")),
        name="diff_attention",
    )(qkv, qkv, qkv, vec(lq1), vec(lk1), vec(lq2), vec(lk2), vec(subln_g))


KV_PAIR = V7X_LANES // SWA_HEAD_DIM
Q_PAIR_WIDTH = KV_PAIR * SWA_GROUP * SWA_HEAD_DIM


def _swa_kernel(sink_ref, q_ref, k_ref, v_ref, o_ref, *, rows):
    p_idx = pl.program_id(1)
    i = pl.program_id(2)
    w, d, g = SWA_WINDOW, SWA_HEAD_DIM, SWA_GROUP
    log2e = math.log2(math.e)
    units = [(r, hh) for r in range(rows // w) for hh in range(KV_PAIR)]

    def positions(r):
        base = i * rows + r * w
        return base, pl.multiple_of(jnp.maximum(base - w, 0), w)

    def scores(r, hh):
        _, start = positions(r)
        kk = k_ref[pl.ds(start, 2 * w), :]
        q = q_ref[r * w:(r + 1) * w, hh * g * d:(hh + 1) * g * d]
        qt = q.astype(F32).T.astype(BF16)
        zeros = jnp.zeros((d, w), BF16)
        cols = [jnp.concatenate([qt[gg * d:(gg + 1) * d], zeros] if hh == 0
                                else [zeros, qt[gg * d:(gg + 1) * d]], axis=0)
                for gg in range(g)]
        return jnp.dot(kk, jnp.concatenate(cols, axis=1), preferred_element_type=F32)

    def finish(r, hh, s):
        base, start = positions(r)
        k_pos = start + lax.broadcasted_iota(jnp.int32, (2 * w, w), 0)
        q_pos = base + lax.broadcasted_iota(jnp.int32, (2 * w, w), 1)
        mask = (k_pos <= q_pos) & (q_pos - k_pos < w)
        s = jnp.where(jnp.concatenate([mask] * g, axis=1), s, NEG_INF)
        head0 = (p_idx * KV_PAIR + hh) * g
        sink = jnp.concatenate([jnp.full((1, w), sink_ref[head0 + gg] * log2e, F32)
                                for gg in range(g)], axis=1)
        m = jnp.maximum(jnp.max(s, axis=0, keepdims=True), sink)
        e = jnp.exp2(s - m)
        denom = jnp.sum(e, axis=0, keepdims=True) + jnp.exp2(sink - m)
        vv = v_ref[pl.ds(start, 2 * w), :][:, hh * d:(hh + 1) * d]
        ot = lax.dot_general(vv, e.astype(BF16), (((0,), (0,)), ((), ())),
                             preferred_element_type=F32)
        ot = ot * (1.0 / denom)
        o = jnp.concatenate([ot[:, gg * w:(gg + 1) * w] for gg in range(g)], axis=0).T
        o_ref[r * w:(r + 1) * w, hh * g * d:(hh + 1) * g * d] = o.astype(o_ref.dtype)

    s_next = scores(*units[0])
    for n, (r, hh) in enumerate(units):
        s_cur = s_next
        if n + 1 < len(units):
            s_next = scores(*units[n + 1])
        finish(r, hh, s_cur)


def swa_attention(qkv, sinks, batch, seq_len, rows=512):
    t = qkv.shape[0]
    nq = seq_len // rows
    n_pairs = SWA_KV_HEADS // KV_PAIR
    k_blk0 = SWA_Q_WIDTH // V7X_LANES
    v_blk0 = (SWA_Q_WIDTH + SWA_KV_WIDTH) // V7X_LANES
    kern = functools.partial(_swa_kernel, rows=rows)
    return pl.pallas_call(
        kern,
        grid=(batch, n_pairs, nq),
        in_specs=[pl.BlockSpec(memory_space=pltpu.SMEM),
                  pl.BlockSpec((rows, Q_PAIR_WIDTH), lambda b, p, i: (b * nq + i, p)),
                  pl.BlockSpec((seq_len, V7X_LANES), lambda b, p, i: (b, k_blk0 + p)),
                  pl.BlockSpec((seq_len, V7X_LANES), lambda b, p, i: (b, v_blk0 + p))],
        out_specs=pl.BlockSpec((rows, Q_PAIR_WIDTH), lambda b, p, i: (b * nq + i, p)),
        out_shape=jax.ShapeDtypeStruct((t, SWA_Q_WIDTH), BF16),
        compiler_params=_compiler_params(("parallel", "parallel", "arbitrary")),
        name="swa_attention",
    )(sinks.astype(F32), qkv, qkv, qkv)


def _softmax_q_scale(head_dim, q_width, total_width):
    c = head_dim ** -0.5 * math.log2(math.e)
    return jnp.concatenate([jnp.full((q_width,), c, F32),
                            jnp.ones((total_width - q_width,), F32)])


def _lambda_init(layer_idx):
    return 0.8 - 0.6 * math.exp(-0.3 * layer_idx)


def kernel(x, norm_mix, norm_mlp, norm_final, w_in_even, w_pool, pool_scale, lambda_q1, lambda_k1, lambda_q2, lambda_k2, subln_g, w_out_even, w_in_odd, b_in_odd, sinks, w_out_odd, b_out_odd, w_up, w_down):
    b, s, d = x.shape
    x = x.reshape(b * s, d)
    w_in_even, w_out_even, w_in_odd, w_out_odd, w_up, w_down, w_pool = (
        w.astype(BF16) for w in (w_in_even, w_out_even, w_in_odd, w_out_odd, w_up, w_down, w_pool))
    diff_q_scale = _softmax_q_scale(DIFF_HEAD_DIM, DIFF_WIDTH, 3 * DIFF_WIDTH)
    swa_q_scale = _softmax_q_scale(SWA_HEAD_DIM, SWA_Q_WIDTH, SWA_Q_WIDTH + 2 * SWA_KV_WIDTH)
    for i in range(DEPTH):
        j = i // 2
        h = rmsnorm(x, norm_mix[i], BF16)
        if i % 2 == 0:
            u = matmul([h], w_in_even, j, n=POOL_WIDTH, out_dtype=F32)
            qkv = matmul([h], w_in_even, j, col0=POOL_WIDTH, col_scale=diff_q_scale)
            o_a = multiscale_pool(u, w_pool[j], pool_scale[j], s)
            o_b = diff_attention(qkv, lambda_q1[j], lambda_k1[j], lambda_q2[j], lambda_k2[j],
                                 subln_g[j], _lambda_init(i), b, s)
            x = matmul([o_a, o_b], w_out_even, j, residual=x, out_dtype=F32)
        else:
            qkv = matmul([h], w_in_odd, j, bias=b_in_odd[j], col_scale=swa_q_scale)
            o_c = swa_attention(qkv, sinks[j], b, s)
            x = matmul([o_c], w_out_odd, j, bias=b_out_odd[j], residual=x, out_dtype=F32)
        h = rmsnorm(x, norm_mlp[i], BF16)
        a = matmul([h], w_up, i, relu2=True)
        x = matmul([a], w_down, i, residual=x, out_dtype=F32, bk=4096)
    return rmsnorm(x, norm_final, F32).reshape(b, s, d)
```

```python
import functools
import math

import jax
import jax.numpy as jnp
from jax import lax
from jax.experimental import pallas as pl
from jax.experimental.pallas import tpu as pltpu

D_MODEL = 4096
DEPTH = 4
POOL_WIDTH = D_MODEL // 2
POOL_WINDOWS = (2, 4, 8, 16)
POOL_GROUP = POOL_WIDTH // len(POOL_WINDOWS)
MAX_WINDOW = max(POOL_WINDOWS)
DIFF_HEAD_DIM = 64
DIFF_WIDTH = D_MODEL - POOL_WIDTH
DIFF_HEADS = DIFF_WIDTH // (2 * DIFF_HEAD_DIM)
SWA_HEADS = 64
SWA_KV_HEADS = 8
SWA_HEAD_DIM = D_MODEL // SWA_HEADS
SWA_GROUP = SWA_HEADS // SWA_KV_HEADS
SWA_WINDOW = 128
SWA_Q_WIDTH = SWA_HEADS * SWA_HEAD_DIM
SWA_KV_WIDTH = SWA_KV_HEADS * SWA_HEAD_DIM
D_FF = 4 * D_MODEL
RMS_EPS = 1e-5

V7X_LANES = 128
V7X_VMEM_BYTES = 64 * 1024 * 1024
VMEM_LIMIT_BYTES = V7X_VMEM_BYTES - 8 * 1024 * 1024

V7X_MXU_COLS = 256
ACC_STRIP = V7X_MXU_COLS

BF16 = jnp.bfloat16
F32 = jnp.float32
NEG_INF = float("-inf")


def _compiler_params(semantics):
    return pltpu.CompilerParams(dimension_semantics=semantics,
                                vmem_limit_bytes=VMEM_LIMIT_BYTES)


def _rmsnorm_kernel(x_ref, g_ref, o_ref):
    x = x_ref[...]
    ms = jnp.mean(x * x, axis=-1, keepdims=True)
    o_ref[...] = (x * lax.rsqrt(ms + RMS_EPS) * g_ref[...]).astype(o_ref.dtype)


def rmsnorm(x, g, out_dtype, block_rows=256):
    t, d = x.shape
    return pl.pallas_call(
        _rmsnorm_kernel,
        grid=(t // block_rows,),
        in_specs=[pl.BlockSpec((block_rows, d), lambda i: (i, 0)),
                  pl.BlockSpec((1, d), lambda i: (0, 0))],
        out_specs=pl.BlockSpec((block_rows, d), lambda i: (i, 0)),
        out_shape=jax.ShapeDtypeStruct((t, d), out_dtype),
        compiler_params=_compiler_params(("parallel",)),
        name="rmsnorm",
    )(x, g.reshape(1, d))


def _matmul_kernel(*refs, nk, n_parts, relu2, has_bias, has_scale, has_res, has_ssq, emit_norm,
                   norm_dim):
    it = iter(refs)
    x_refs = [next(it) for _ in range(n_parts)]
    w_ref = next(it)
    q_ref = next(it) if has_ssq else None
    b_ref = next(it) if has_bias else None
    c_ref = next(it) if has_scale else None
    r_ref = next(it) if has_res else None
    g_ref = next(it) if emit_norm else None
    o_ref = next(it)
    og_ref, oq_ref = (next(it), next(it)) if emit_norm else (None, None)
    acc_ref = next(it) if nk > 1 else None

    def epilogue(acc):
        if has_ssq:
            ssq = q_ref[:, 0:V7X_LANES]
            for t in range(1, q_ref.shape[1] // V7X_LANES):
                ssq = ssq + q_ref[:, t * V7X_LANES:(t + 1) * V7X_LANES]
            rms = lax.rsqrt(ssq * (1.0 / norm_dim) + RMS_EPS)
            acc = jnp.concatenate(
                [acc[:, c:c + V7X_LANES] * rms for c in range(0, acc.shape[1], V7X_LANES)], axis=1)
        if has_bias:
            acc = acc + b_ref[...]
        if has_scale:
            acc = acc * c_ref[...]
        if relu2:
            a = jnp.maximum(acc, 0.0)
            acc = a * a
        if has_res:
            acc = acc + r_ref[...]
        o_ref[...] = acc.astype(o_ref.dtype)
        if emit_norm:
            og_ref[...] = (acc * g_ref[...]).astype(og_ref.dtype)
            oq_ref[...] = jnp.broadcast_to(jnp.sum(acc * acc, axis=1, keepdims=True), oq_ref.shape)

    if nk == 1:
        kp = w_ref.shape[0] // n_parts
        acc = jnp.dot(x_refs[0][...], w_ref[0:kp, :], preferred_element_type=F32)
        for p in range(1, n_parts):
            acc += jnp.dot(x_refs[p][...], w_ref[p * kp:(p + 1) * kp, :],
                           preferred_element_type=F32)
        epilogue(acc)
        return

    k = pl.program_id(2)

    @pl.when(k == 0)
    def _():
        acc_ref[...] = jnp.zeros_like(acc_ref)

    strip = min(ACC_STRIP, acc_ref.shape[1])
    for c0 in range(0, acc_ref.shape[1], strip):
        acc_ref[:, c0:c0 + strip] += jnp.dot(x_refs[0][...], w_ref[:, c0:c0 + strip],
                                             preferred_element_type=F32)

    @pl.when(k == nk - 1)
    def _():
        epilogue(acc_ref[...])


def matmul(x_parts, w_stack, layer, *, col0=0, n=None, row_ssq=None, bias=None, col_scale=None,
           residual=None, relu2=False, next_gain=None, out_dtype=BF16,
           bm=1024, bn=1024, bk=None):
    x_parts = list(x_parts)
    m = x_parts[0].shape[0]
    kdim = sum(xp.shape[1] for xp in x_parts)
    n = w_stack.shape[2] - col0 if n is None else n
    bm, bn = math.gcd(bm, m), math.gcd(math.gcd(bn, n), col0) if col0 else math.gcd(bn, n)
    bk = kdim if bk is None else math.gcd(bk, kdim)
    nk = kdim // bk
    assert nk == 1 or len(x_parts) == 1
    jb0 = col0 // bn
    in_specs = [pl.BlockSpec((bm, bk // len(x_parts)), lambda i, j, k: (i, k))
                for _ in x_parts]
    in_specs.append(pl.BlockSpec((None, bk, bn), lambda i, j, k: (layer, k, jb0 + j)))
    args = x_parts + [w_stack]
    if row_ssq is not None:
        in_specs.append(pl.BlockSpec((bm, row_ssq.shape[1]), lambda i, j, k: (i, 0)))
        args.append(row_ssq)
    if bias is not None:
        in_specs.append(pl.BlockSpec((1, bn), lambda i, j, k: (0, j)))
        args.append(bias.reshape(1, n).astype(F32))
    if col_scale is not None:
        in_specs.append(pl.BlockSpec((1, bn), lambda i, j, k: (0, j)))
        args.append(col_scale.reshape(1, n).astype(F32))
    if residual is not None:
        in_specs.append(pl.BlockSpec((bm, bn), lambda i, j, k: (i, j)))
        args.append(residual)
    out_spec = pl.BlockSpec((bm, bn), lambda i, j, k: (i, j))
    out_specs, out_shape = out_spec, jax.ShapeDtypeStruct((m, n), out_dtype)
    if next_gain is not None:
        in_specs.append(pl.BlockSpec((1, bn), lambda i, j, k: (0, j)))
        args.append(next_gain.reshape(1, n).astype(F32))
        out_specs = [out_spec, out_spec, pl.BlockSpec((bm, V7X_LANES), lambda i, j, k: (i, j))]
        out_shape = [out_shape, jax.ShapeDtypeStruct((m, n), BF16),
                     jax.ShapeDtypeStruct((m, n // bn * V7X_LANES), F32)]
    kern = functools.partial(_matmul_kernel, nk=nk, n_parts=len(x_parts), relu2=relu2,
                             has_bias=bias is not None, has_scale=col_scale is not None,
                             has_res=residual is not None, has_ssq=row_ssq is not None,
                             emit_norm=next_gain is not None, norm_dim=kdim)
    return pl.pallas_call(
        kern,
        grid=(m // bm, n // bn, nk),
        in_specs=in_specs,
        out_specs=out_specs,
        out_shape=out_shape,
        scratch_shapes=[pltpu.VMEM((bm, bn), F32)] if nk > 1 else [],
        compiler_params=_compiler_params(("parallel", "parallel", "arbitrary")),
        name="matmul",
    )(*args)


def _pool_kernel(u_ref, halo_ref, w_ref, s_ref, o_ref, *, rows, blocks_per_seq):
    i = pl.program_id(0)
    seq_blk = i % blocks_per_seq
    halo = jnp.where(seq_blk == 0, 0.0, halo_ref[...])
    t = seq_blk * rows + lax.broadcasted_iota(jnp.int32, (rows, 1), 0)
    for g, win in enumerate(POOL_WINDOWS):
        lo, hi = g * POOL_GROUP, (g + 1) * POOL_GROUP
        cur = u_ref[:, lo:hi]
        ext = jnp.concatenate([halo[:, lo:hi], cur], axis=0)
        win_sum = cur
        for j in range(1, win):
            win_sum = win_sum + ext[MAX_WINDOW - j:MAX_WINDOW - j + rows]
        count = jnp.minimum(t + 1, win).astype(F32)
        p = (win_sum / count - cur).astype(BF16)
        y = jnp.dot(p, w_ref[g], preferred_element_type=F32)
        o_ref[:, lo:hi] = (y * s_ref[:, lo:hi]).astype(o_ref.dtype)


def multiscale_pool(u, w_pool, pool_scale, seq_len, rows=512):
    t, c = u.shape
    halo_per_blk = rows // MAX_WINDOW
    kern = functools.partial(_pool_kernel, rows=rows, blocks_per_seq=seq_len // rows)
    return pl.pallas_call(
        kern,
        grid=(t // rows,),
        in_specs=[pl.BlockSpec((rows, c), lambda i: (i, 0)),
                  pl.BlockSpec((MAX_WINDOW, c),
                               lambda i: (jnp.maximum(i * halo_per_blk - 1, 0), 0)),
                  pl.BlockSpec((len(POOL_WINDOWS), POOL_GROUP, POOL_GROUP),
                               lambda i: (0, 0, 0)),
                  pl.BlockSpec((1, c), lambda i: (0, 0))],
        out_specs=pl.BlockSpec((rows, c), lambda i: (i, 0)),
        out_shape=jax.ShapeDtypeStruct((t, c), BF16),
        compiler_params=_compiler_params(("parallel",)),
        name="multiscale_pool",
    )(u, u, w_pool, pool_scale.reshape(1, c))


def _diff_attn_kernel(q_ref, k_ref, v_ref, lq1_ref, lk1_ref, lq2_ref, lk2_ref, g_ref,
                      o_ref, sa_scr, sb_scr, *, bq, bk, lam_init):
    i = pl.program_id(2)
    d = DIFF_HEAD_DIM
    qt = q_ref[...].astype(F32).T.astype(BF16)
    feat = lax.broadcasted_iota(jnp.int32, qt.shape, 0)
    zero = jnp.zeros_like(qt)
    qt_maps = (jnp.where(feat < d, qt, zero), jnp.where(feat >= d, qt, zero))

    def scores(j, s_scr):
        kb = k_ref[pl.ds(pl.multiple_of(j * bk, bk), bk), :]
        for c in range(2):
            s_scr[c] = jnp.dot(kb, qt_maps[c], preferred_element_type=F32)

    def update(j, s_scr, stats, key_offset=None):
        vb = v_ref[pl.ds(pl.multiple_of(j * bk, bk), bk), :]
        out = []
        for c in range(2):
            m_prev, l_prev, acc_prev = stats[c]
            s = s_scr[c]
            if key_offset is not None:
                key = lax.broadcasted_iota(jnp.int32, s.shape, 0) + key_offset
                qry = lax.broadcasted_iota(jnp.int32, s.shape, 1)
                s = jnp.where(key <= qry, s, NEG_INF)
            m_new = jnp.maximum(m_prev, jnp.max(s, axis=0, keepdims=True))
            alpha = jnp.exp2(m_prev - m_new)
            p = jnp.exp2(s - m_new)
            l_new = alpha * l_prev + jnp.sum(p, axis=0, keepdims=True)
            pv = lax.dot_general(vb, p.astype(BF16), (((0,), (0,)), ((), ())),
                                 preferred_element_type=F32)
            out.append((m_new, l_new, alpha * acc_prev + pv))
        return tuple(out)

    init_one = (jnp.full((1, bq), NEG_INF, F32), jnp.zeros((1, bq), F32),
                jnp.zeros((2 * d, bq), F32))

    scores(0, sa_scr)

    def pair(jj, stats):
        j = 2 * jj
        scores(j + 1, sb_scr)
        stats = update(j, sa_scr, stats)
        scores(j + 2, sa_scr)
        return update(j + 1, sb_scr, stats)

    stats = lax.fori_loop(0, i, pair, (init_one, init_one))
    scores(2 * i + 1, sb_scr)
    stats = update(2 * i, sa_scr, stats, key_offset=0)
    (_, l1, acc1), (_, l2, acc2) = update(2 * i + 1, sb_scr, stats, key_offset=bk)

    lam = (jnp.exp(jnp.sum(lq1_ref[...] * lk1_ref[...], axis=-1, keepdims=True))
           - jnp.exp(jnp.sum(lq2_ref[...] * lk2_ref[...], axis=-1, keepdims=True))
           + lam_init)
    o = (acc1 / l1 - lam * (acc2 / l2)).T
    ms = jnp.mean(o * o, axis=-1, keepdims=True)
    o = o * lax.rsqrt(ms + RMS_EPS) * g_ref[...] * (1.0 - lam_init)
    o_ref[...] = o.astype(o_ref.dtype)


def diff_attention(qkv, lq1, lk1, lq2, lk2, subln_g, lam_init, batch, seq_len, bk=512):
    t = qkv.shape[0]
    bq = 2 * bk
    nq = seq_len // bq
    hw = 2 * DIFF_HEAD_DIM
    vec = lambda a: a.reshape(1, -1).astype(F32)
    vec_spec = lambda n: pl.BlockSpec((1, n), lambda b, h, i: (0, 0))
    kern = functools.partial(_diff_attn_kernel, bq=bq, bk=bk, lam_init=lam_init)
    return pl.pallas_call(
        kern,
        grid=(batch, DIFF_HEADS, nq),
        in_specs=[pl.BlockSpec((bq, hw), lambda b, h, i: (b * nq + i, h)),
                  pl.BlockSpec((seq_len, hw), lambda b, h, i: (b, DIFF_HEADS + h)),
                  pl.BlockSpec((seq_len, hw), lambda b, h, i: (b, 2 * DIFF_HEADS + h)),
                  vec_spec(DIFF_HEAD_DIM), vec_spec(DIFF_HEAD_DIM),
                  vec_spec(DIFF_HEAD_DIM), vec_spec(DIFF_HEAD_DIM), vec_spec(hw)],
        out_specs=pl.BlockSpec((bq, hw), lambda b, h, i: (b * nq + i, h)),
        out_shape=jax.ShapeDtypeStruct((t, DIFF_WIDTH), BF16),
        scratch_shapes=[pltpu.VMEM((2, bk, bq), F32)] * 2,
        compiler_params=_compiler_params(("parallel", "parallel", "arbitrary")),
        name="diff_attention",
    )(qkv, qkv, qkv, vec(lq1), vec(lk1), vec(lq2), vec(lk2), vec(subln_g))


KV_PAIR = V7X_LANES // SWA_HEAD_DIM
Q_PAIR_WIDTH = KV_PAIR * SWA_GROUP * SWA_HEAD_DIM


def _swa_kernel(sink_ref, q_ref, k_ref, v_ref, o_ref, *, rows):
    p_idx = pl.program_id(1)
    i = pl.program_id(2)
    w, d, g = SWA_WINDOW, SWA_HEAD_DIM, SWA_GROUP
    log2e = math.log2(math.e)
    units = [(r, hh) for r in range(rows // w) for hh in range(KV_PAIR)]

    def positions(r):
        base = i * rows + r * w
        return base, pl.multiple_of(jnp.maximum(base - w, 0), w)

    def scores(r, hh):
        _, start = positions(r)
        kk = k_ref[pl.ds(start, 2 * w), :]
        q = q_ref[r * w:(r + 1) * w, hh * g * d:(hh + 1) * g * d]
        qt = q.astype(F32).T.astype(BF16)
        zeros = jnp.zeros((d, w), BF16)
        cols = [jnp.concatenate([qt[gg * d:(gg + 1) * d], zeros] if hh == 0
                                else [zeros, qt[gg * d:(gg + 1) * d]], axis=0)
                for gg in range(g)]
        return jnp.dot(kk, jnp.concatenate(cols, axis=1), preferred_element_type=F32)

    def finish(r, hh, s):
        base, start = positions(r)
        k_pos = start + lax.broadcasted_iota(jnp.int32, (2 * w, w), 0)
        q_pos = base + lax.broadcasted_iota(jnp.int32, (2 * w, w), 1)
        mask = (k_pos <= q_pos) & (q_pos - k_pos < w)
        s = jnp.where(jnp.concatenate([mask] * g, axis=1), s, NEG_INF)
        head0 = (p_idx * KV_PAIR + hh) * g
        sink = jnp.concatenate([jnp.full((1, w), sink_ref[head0 + gg] * log2e, F32)
                                for gg in range(g)], axis=1)
        m = jnp.maximum(jnp.max(s, axis=0, keepdims=True), sink)
        e = jnp.exp2(s - m)
        denom = jnp.sum(e, axis=0, keepdims=True) + jnp.exp2(sink - m)
        vv = v_ref[pl.ds(start, 2 * w), :][:, hh * d:(hh + 1) * d]
        ot = lax.dot_general(vv, e.astype(BF16), (((0,), (0,)), ((), ())),
                             preferred_element_type=F32)
        ot = ot * (1.0 / denom)
        o = jnp.concatenate([ot[:, gg * w:(gg + 1) * w] for gg in range(g)], axis=0).T
        o_ref[r * w:(r + 1) * w, hh * g * d:(hh + 1) * g * d] = o.astype(o_ref.dtype)

    s_next = scores(*units[0])
    for n, (r, hh) in enumerate(units):
        s_cur = s_next
        if n + 1 < len(units):
            s_next = scores(*units[n + 1])
        finish(r, hh, s_cur)


def swa_attention(qkv, sinks, batch, seq_len, rows=512):
    t = qkv.shape[0]
    nq = seq_len // rows
    n_pairs = SWA_KV_HEADS // KV_PAIR
    k_blk0 = SWA_Q_WIDTH // V7X_LANES
    v_blk0 = (SWA_Q_WIDTH + SWA_KV_WIDTH) // V7X_LANES
    kern = functools.partial(_swa_kernel, rows=rows)
    return pl.pallas_call(
        kern,
        grid=(batch, n_pairs, nq),
        in_specs=[pl.BlockSpec(memory_space=pltpu.SMEM),
                  pl.BlockSpec((rows, Q_PAIR_WIDTH), lambda b, p, i: (b * nq + i, p)),
                  pl.BlockSpec((seq_len, V7X_LANES), lambda b, p, i: (b, k_blk0 + p)),
                  pl.BlockSpec((seq_len, V7X_LANES), lambda b, p, i: (b, v_blk0 + p))],
        out_specs=pl.BlockSpec((rows, Q_PAIR_WIDTH), lambda b, p, i: (b * nq + i, p)),
        out_shape=jax.ShapeDtypeStruct((t, SWA_Q_WIDTH), BF16),
        compiler_params=_compiler_params(("parallel", "parallel", "arbitrary")),
        name="swa_attention",
    )(sinks.astype(F32), qkv, qkv, qkv)


def _softmax_q_scale(head_dim, q_width, total_width):
    c = head_dim ** -0.5 * math.log2(math.e)
    return jnp.concatenate([jnp.full((q_width,), c, F32),
                            jnp.ones((total_width - q_width,), F32)])


def _lambda_init(layer_idx):
    return 0.8 - 0.6 * math.exp(-0.3 * layer_idx)


def kernel(x, norm_mix, norm_mlp, norm_final, w_in_even, w_pool, pool_scale, lambda_q1, lambda_k1, lambda_q2, lambda_k2, subln_g, w_out_even, w_in_odd, b_in_odd, sinks, w_out_odd, b_out_odd, w_up, w_down):
    b, s, d = x.shape
    x = x.reshape(b * s, d)
    w_in_even, w_out_even, w_in_odd, w_out_odd, w_up, w_down, w_pool = (
        w.astype(BF16) for w in (w_in_even, w_out_even, w_in_odd, w_out_odd, w_up, w_down, w_pool))
    diff_q_scale = _softmax_q_scale(DIFF_HEAD_DIM, DIFF_WIDTH, 3 * DIFF_WIDTH)
    swa_q_scale = _softmax_q_scale(SWA_HEAD_DIM, SWA_Q_WIDTH, SWA_Q_WIDTH + 2 * SWA_KV_WIDTH)
    xg, ssq = rmsnorm(x, norm_mix[0], BF16), None
    for i in range(DEPTH):
        j = i // 2
        if i % 2 == 0:
            u = matmul([xg], w_in_even, j, n=POOL_WIDTH, row_ssq=ssq, out_dtype=F32)
            qkv = matmul([xg], w_in_even, j, col0=POOL_WIDTH, row_ssq=ssq, col_scale=diff_q_scale)
            o_a = multiscale_pool(u, w_pool[j], pool_scale[j], s)
            o_b = diff_attention(qkv, lambda_q1[j], lambda_k1[j], lambda_q2[j], lambda_k2[j],
                                 subln_g[j], _lambda_init(i), b, s)
            x, xg, ssq = matmul([o_a, o_b], w_out_even, j, residual=x, next_gain=norm_mlp[i],
                                out_dtype=F32, bn=512)
        else:
            qkv = matmul([xg], w_in_odd, j, row_ssq=ssq, bias=b_in_odd[j], col_scale=swa_q_scale)
            o_c = swa_attention(qkv, sinks[j], b, s)
            x, xg, ssq = matmul([o_c], w_out_odd, j, bias=b_out_odd[j], residual=x,
                                next_gain=norm_mlp[i], out_dtype=F32, bn=512)
        a = matmul([xg], w_up, i, row_ssq=ssq, relu2=True)
        if i + 1 < DEPTH:
            x, xg, ssq = matmul([a], w_down, i, residual=x, next_gain=norm_mix[i + 1],
                                out_dtype=F32, bk=2048)
        else:
            x = matmul([a], w_down, i, residual=x, out_dtype=F32, bk=4096)
    return rmsnorm(x, norm_final, F32).reshape(b, s, d)
```

```python
import functools
import math

import jax
import jax.numpy as jnp
from jax import lax
from jax.experimental import pallas as pl
from jax.experimental.pallas import tpu as pltpu

D_MODEL = 4096
DEPTH = 4
POOL_WIDTH = D_MODEL // 2
POOL_WINDOWS = (2, 4, 8, 16)
POOL_GROUP = POOL_WIDTH // len(POOL_WINDOWS)
MAX_WINDOW = max(POOL_WINDOWS)
DIFF_HEAD_DIM = 64
DIFF_WIDTH = D_MODEL - POOL_WIDTH
DIFF_HEADS = DIFF_WIDTH // (2 * DIFF_HEAD_DIM)
SWA_HEADS = 64
SWA_KV_HEADS = 8
SWA_HEAD_DIM = D_MODEL // SWA_HEADS
SWA_GROUP = SWA_HEADS // SWA_KV_HEADS
SWA_WINDOW = 128
SWA_Q_WIDTH = SWA_HEADS * SWA_HEAD_DIM
SWA_KV_WIDTH = SWA_KV_HEADS * SWA_HEAD_DIM
D_FF = 4 * D_MODEL
RMS_EPS = 1e-5

V7X_LANES = 128
V7X_VMEM_BYTES = 64 * 1024 * 1024
VMEM_LIMIT_BYTES = V7X_VMEM_BYTES - 2 * 1024 * 1024

V7X_MXU_COLS = 256
ACC_STRIP = V7X_MXU_COLS

BF16 = jnp.bfloat16
F32 = jnp.float32
NEG_INF = float("-inf")


def _compiler_params(semantics):
    return pltpu.CompilerParams(dimension_semantics=semantics,
                                vmem_limit_bytes=VMEM_LIMIT_BYTES)


def _rmsnorm_kernel(x_ref, g_ref, o_ref):
    x = x_ref[...]
    ms = jnp.mean(x * x, axis=-1, keepdims=True)
    o_ref[...] = (x * lax.rsqrt(ms + RMS_EPS) * g_ref[...]).astype(o_ref.dtype)


def rmsnorm(x, g, out_dtype, block_rows=256):
    t, d = x.shape
    return pl.pallas_call(
        _rmsnorm_kernel,
        grid=(t // block_rows,),
        in_specs=[pl.BlockSpec((block_rows, d), lambda i: (i, 0)),
                  pl.BlockSpec((1, d), lambda i: (0, 0))],
        out_specs=pl.BlockSpec((block_rows, d), lambda i: (i, 0)),
        out_shape=jax.ShapeDtypeStruct((t, d), out_dtype),
        compiler_params=_compiler_params(("parallel",)),
        name="rmsnorm",
    )(x, g.reshape(1, d))


def _matmul_kernel(*refs, nk, n_parts, relu2, has_bias, has_scale, has_res, has_ssq, emit_norm,
                   norm_dim):
    it = iter(refs)
    x_refs = [next(it) for _ in range(n_parts)]
    w_ref = next(it)
    q_ref = next(it) if has_ssq else None
    b_ref = next(it) if has_bias else None
    c_ref = next(it) if has_scale else None
    r_ref = next(it) if has_res else None
    g_ref = next(it) if emit_norm else None
    o_ref = next(it)
    og_ref, oq_ref = (next(it), next(it)) if emit_norm else (None, None)
    acc_ref = next(it) if nk > 1 else None

    def epilogue(acc):
        if has_ssq:
            ssq = q_ref[:, 0:V7X_LANES]
            for t in range(1, q_ref.shape[1] // V7X_LANES):
                ssq = ssq + q_ref[:, t * V7X_LANES:(t + 1) * V7X_LANES]
            rms = lax.rsqrt(ssq * (1.0 / norm_dim) + RMS_EPS)
            acc = jnp.concatenate(
                [acc[:, c:c + V7X_LANES] * rms for c in range(0, acc.shape[1], V7X_LANES)], axis=1)
        if has_bias:
            acc = acc + b_ref[...]
        if has_scale:
            acc = acc * c_ref[...]
        if relu2:
            a = jnp.maximum(acc, 0.0)
            acc = a * a
        if has_res:
            acc = acc + r_ref[...]
        o_ref[...] = acc.astype(o_ref.dtype)
        if emit_norm:
            og_ref[...] = (acc * g_ref[...]).astype(og_ref.dtype)
            oq_ref[...] = jnp.broadcast_to(jnp.sum(acc * acc, axis=1, keepdims=True), oq_ref.shape)

    if nk == 1:
        kp = w_ref.shape[0] // n_parts
        acc = jnp.dot(x_refs[0][...], w_ref[0:kp, :], preferred_element_type=F32)
        for p in range(1, n_parts):
            acc += jnp.dot(x_refs[p][...], w_ref[p * kp:(p + 1) * kp, :],
                           preferred_element_type=F32)
        epilogue(acc)
        return

    k = pl.program_id(2)

    @pl.when(k == 0)
    def _():
        acc_ref[...] = jnp.zeros_like(acc_ref)

    strip = min(ACC_STRIP, acc_ref.shape[1])
    for c0 in range(0, acc_ref.shape[1], strip):
        acc_ref[:, c0:c0 + strip] += jnp.dot(x_refs[0][...], w_ref[:, c0:c0 + strip],
                                             preferred_element_type=F32)

    @pl.when(k == nk - 1)
    def _():
        epilogue(acc_ref[...])


def matmul(x_parts, w_stack, layer, *, col0=0, n=None, row_ssq=None, bias=None, col_scale=None,
           residual=None, relu2=False, next_gain=None, out_dtype=BF16,
           bm=1024, bn=1024, bk=None):
    x_parts = list(x_parts)
    m = x_parts[0].shape[0]
    kdim = sum(xp.shape[1] for xp in x_parts)
    n = w_stack.shape[2] - col0 if n is None else n
    bm, bn = math.gcd(bm, m), math.gcd(math.gcd(bn, n), col0) if col0 else math.gcd(bn, n)
    bk = kdim if bk is None else math.gcd(bk, kdim)
    nk = kdim // bk
    assert nk == 1 or len(x_parts) == 1
    jb0 = col0 // bn
    in_specs = [pl.BlockSpec((bm, bk // len(x_parts)), lambda i, j, k: (i, k))
                for _ in x_parts]
    in_specs.append(pl.BlockSpec((None, bk, bn), lambda i, j, k: (layer, k, jb0 + j)))
    args = x_parts + [w_stack]
    if row_ssq is not None:
        in_specs.append(pl.BlockSpec((bm, row_ssq.shape[1]), lambda i, j, k: (i, 0)))
        args.append(row_ssq)
    if bias is not None:
        in_specs.append(pl.BlockSpec((1, bn), lambda i, j, k: (0, j)))
        args.append(bias.reshape(1, n).astype(F32))
    if col_scale is not None:
        in_specs.append(pl.BlockSpec((1, bn), lambda i, j, k: (0, j)))
        args.append(col_scale.reshape(1, n).astype(F32))
    if residual is not None:
        in_specs.append(pl.BlockSpec((bm, bn), lambda i, j, k: (i, j)))
        args.append(residual)
    out_spec = pl.BlockSpec((bm, bn), lambda i, j, k: (i, j))
    out_specs, out_shape = out_spec, jax.ShapeDtypeStruct((m, n), out_dtype)
    if next_gain is not None:
        in_specs.append(pl.BlockSpec((1, bn), lambda i, j, k: (0, j)))
        args.append(next_gain.reshape(1, n).astype(F32))
        out_specs = [out_spec, out_spec, pl.BlockSpec((bm, V7X_LANES), lambda i, j, k: (i, j))]
        out_shape = [out_shape, jax.ShapeDtypeStruct((m, n), BF16),
                     jax.ShapeDtypeStruct((m, n // bn * V7X_LANES), F32)]
    kern = functools.partial(_matmul_kernel, nk=nk, n_parts=len(x_parts), relu2=relu2,
                             has_bias=bias is not None, has_scale=col_scale is not None,
                             has_res=residual is not None, has_ssq=row_ssq is not None,
                             emit_norm=next_gain is not None, norm_dim=kdim)
    return pl.pallas_call(
        kern,
        grid=(m // bm, n // bn, nk),
        in_specs=in_specs,
        out_specs=out_specs,
        out_shape=out_shape,
        scratch_shapes=[pltpu.VMEM((bm, bn), F32)] if nk > 1 else [],
        compiler_params=_compiler_params(("parallel", "parallel", "arbitrary")),
        name="matmul",
    )(*args)


def _pool_kernel(u_ref, halo_ref, w_ref, s_ref, o_ref, *, rows, blocks_per_seq):
    i = pl.program_id(0)
    seq_blk = i % blocks_per_seq
    halo = jnp.where(seq_blk == 0, 0.0, halo_ref[...])
    t = seq_blk * rows + lax.broadcasted_iota(jnp.int32, (rows, 1), 0)
    for g, win in enumerate(POOL_WINDOWS):
        lo, hi = g * POOL_GROUP, (g + 1) * POOL_GROUP
        cur = u_ref[:, lo:hi]
        ext = jnp.concatenate([halo[:, lo:hi], cur], axis=0)
        win_sum = cur
        for j in range(1, win):
            win_sum = win_sum + ext[MAX_WINDOW - j:MAX_WINDOW - j + rows]
        count = jnp.minimum(t + 1, win).astype(F32)
        p = (win_sum / count - cur).astype(BF16)
        y = jnp.dot(p, w_ref[g], preferred_element_type=F32)
        o_ref[:, lo:hi] = (y * s_ref[:, lo:hi]).astype(o_ref.dtype)


def multiscale_pool(u, w_pool, pool_scale, seq_len, rows=512):
    t, c = u.shape
    halo_per_blk = rows // MAX_WINDOW
    kern = functools.partial(_pool_kernel, rows=rows, blocks_per_seq=seq_len // rows)
    return pl.pallas_call(
        kern,
        grid=(t // rows,),
        in_specs=[pl.BlockSpec((rows, c), lambda i: (i, 0)),
                  pl.BlockSpec((MAX_WINDOW, c),
                               lambda i: (jnp.maximum(i * halo_per_blk - 1, 0), 0)),
                  pl.BlockSpec((len(POOL_WINDOWS), POOL_GROUP, POOL_GROUP),
                               lambda i: (0, 0, 0)),
                  pl.BlockSpec((1, c), lambda i: (0, 0))],
        out_specs=pl.BlockSpec((rows, c), lambda i: (i, 0)),
        out_shape=jax.ShapeDtypeStruct((t, c), BF16),
        compiler_params=_compiler_params(("parallel",)),
        name="multiscale_pool",
    )(u, u, w_pool, pool_scale.reshape(1, c))


def _diff_attn_kernel(q_ref, k_ref, v_ref, lq1_ref, lk1_ref, lq2_ref, lk2_ref, g_ref,
                      o_ref, sa_scr, sb_scr, *, bq, bk, lam_init):
    i = pl.program_id(2)
    d = DIFF_HEAD_DIM
    qt = q_ref[...].astype(F32).T.astype(BF16)
    feat = lax.broadcasted_iota(jnp.int32, qt.shape, 0)
    zero = jnp.zeros_like(qt)
    qt_maps = (jnp.where(feat < d, qt, zero), jnp.where(feat >= d, qt, zero))

    def scores(j, s_scr, q0=0):
        kb = k_ref[pl.ds(pl.multiple_of(j * bk, bk), bk), :]
        for c in range(2):
            s_scr[c, :, q0:] = jnp.dot(kb, qt_maps[c][:, q0:], preferred_element_type=F32)

    def softmax_pv(s, vb, m_prev, l_prev, acc_prev, causal):
        if causal:
            key = lax.broadcasted_iota(jnp.int32, s.shape, 0)
            qry = lax.broadcasted_iota(jnp.int32, s.shape, 1)
            s = jnp.where(key <= qry, s, NEG_INF)
        m_new = jnp.maximum(m_prev, jnp.max(s, axis=0, keepdims=True))
        alpha = jnp.exp2(m_prev - m_new)
        p = jnp.exp2(s - m_new)
        l_new = alpha * l_prev + jnp.sum(p, axis=0, keepdims=True)
        pv = lax.dot_general(vb, p.astype(BF16), (((0,), (0,)), ((), ())),
                             preferred_element_type=F32)
        return m_new, l_new, alpha * acc_prev + pv

    def update(j, s_scr, stats, halves=None):
        vb = v_ref[pl.ds(pl.multiple_of(j * bk, bk), bk), :]
        out = []
        for c in range(2):
            if halves is None:
                out.append(softmax_pv(s_scr[c], vb, *stats[c], False))
                continue
            parts = []
            for h, mode in enumerate(halves):
                sl = slice(h * bk, (h + 1) * bk)
                st = tuple(a[:, sl] for a in stats[c])
                parts.append(st if mode == "skip" else
                             softmax_pv(s_scr[c, :, sl], vb, *st, mode == "causal"))
            out.append(tuple(jnp.concatenate(a, axis=1) for a in zip(*parts)))
        return tuple(out)

    init_one = (jnp.full((1, bq), NEG_INF, F32), jnp.zeros((1, bq), F32),
                jnp.zeros((2 * d, bq), F32))

    scores(0, sa_scr)

    def pair(jj, stats):
        j = 2 * jj
        scores(j + 1, sb_scr)
        stats = update(j, sa_scr, stats)
        scores(j + 2, sa_scr)
        return update(j + 1, sb_scr, stats)

    stats = lax.fori_loop(0, i, pair, (init_one, init_one))
    scores(2 * i + 1, sb_scr, q0=bk)
    stats = update(2 * i, sa_scr, stats, halves=("causal", None))
    (_, l1, acc1), (_, l2, acc2) = update(2 * i + 1, sb_scr, stats, halves=("skip", "causal"))

    lam = (jnp.exp(jnp.sum(lq1_ref[...] * lk1_ref[...], axis=-1, keepdims=True))
           - jnp.exp(jnp.sum(lq2_ref[...] * lk2_ref[...], axis=-1, keepdims=True))
           + lam_init)
    o = (acc1 / l1 - lam * (acc2 / l2)).T
    ms = jnp.mean(o * o, axis=-1, keepdims=True)
    o = o * lax.rsqrt(ms + RMS_EPS) * g_ref[...] * (1.0 - lam_init)
    o_ref[...] = o.astype(o_ref.dtype)


def diff_attention(qkv, lq1, lk1, lq2, lk2, subln_g, lam_init, batch, seq_len, bk=512):
    t = qkv.shape[0]
    bq = 2 * bk
    nq = seq_len // bq
    hw = 2 * DIFF_HEAD_DIM
    vec = lambda a: a.reshape(1, -1).astype(F32)
    vec_spec = lambda n: pl.BlockSpec((1, n), lambda b, h, i: (0, 0))
    kern = functools.partial(_diff_attn_kernel, bq=bq, bk=bk, lam_init=lam_init)
    return pl.pallas_call(
        kern,
        grid=(batch, DIFF_HEADS, nq),
        in_specs=[pl.BlockSpec((bq, hw), lambda b, h, i: (b * nq + i, h)),
                  pl.BlockSpec((seq_len, hw), lambda b, h, i: (b, DIFF_HEADS + h)),
                  pl.BlockSpec((seq_len, hw), lambda b, h, i: (b, 2 * DIFF_HEADS + h)),
                  vec_spec(DIFF_HEAD_DIM), vec_spec(DIFF_HEAD_DIM),
                  vec_spec(DIFF_HEAD_DIM), vec_spec(DIFF_HEAD_DIM), vec_spec(hw)],
        out_specs=pl.BlockSpec((bq, hw), lambda b, h, i: (b * nq + i, h)),
        out_shape=jax.ShapeDtypeStruct((t, DIFF_WIDTH), BF16),
        scratch_shapes=[pltpu.VMEM((2, bk, bq), F32)] * 2,
        compiler_params=_compiler_params(("parallel", "parallel", "arbitrary")),
        name="diff_attention",
    )(qkv, qkv, qkv, vec(lq1), vec(lk1), vec(lq2), vec(lk2), vec(subln_g))


KV_PAIR = V7X_LANES // SWA_HEAD_DIM
Q_PAIR_WIDTH = KV_PAIR * SWA_GROUP * SWA_HEAD_DIM


def _swa_kernel(sink_ref, q_ref, k_ref, v_ref, o_ref, *, rows):
    p_idx = pl.program_id(1)
    i = pl.program_id(2)
    w, d, g = SWA_WINDOW, SWA_HEAD_DIM, SWA_GROUP
    log2e = math.log2(math.e)
    units = [(r, hh) for r in range(rows // w) for hh in range(KV_PAIR)]

    def positions(r):
        base = i * rows + r * w
        return base, pl.multiple_of(jnp.maximum(base - w, 0), w)

    def scores(r, hh):
        _, start = positions(r)
        kk = k_ref[pl.ds(start, 2 * w), :]
        q = q_ref[r * w:(r + 1) * w, hh * g * d:(hh + 1) * g * d]
        qt = q.astype(F32).T.astype(BF16)
        zeros = jnp.zeros((d, w), BF16)
        cols = [jnp.concatenate([qt[gg * d:(gg + 1) * d], zeros] if hh == 0
                                else [zeros, qt[gg * d:(gg + 1) * d]], axis=0)
                for gg in range(g)]
        return jnp.dot(kk, jnp.concatenate(cols, axis=1), preferred_element_type=F32)

    def finish(r, hh, s):
        base, start = positions(r)
        k_pos = start + lax.broadcasted_iota(jnp.int32, (2 * w, w), 0)
        q_pos = base + lax.broadcasted_iota(jnp.int32, (2 * w, w), 1)
        mask = (k_pos <= q_pos) & (q_pos - k_pos < w)
        s = jnp.where(jnp.concatenate([mask] * g, axis=1), s, NEG_INF)
        head0 = (p_idx * KV_PAIR + hh) * g
        sink = jnp.concatenate([jnp.full((1, w), sink_ref[head0 + gg] * log2e, F32)
                                for gg in range(g)], axis=1)
        m = jnp.maximum(jnp.max(s, axis=0, keepdims=True), sink)
        e = jnp.exp2(s - m)
        denom = jnp.sum(e, axis=0, keepdims=True) + jnp.exp2(sink - m)
        vv = v_ref[pl.ds(start, 2 * w), :][:, hh * d:(hh + 1) * d]
        ot = lax.dot_general(vv, e.astype(BF16), (((0,), (0,)), ((), ())),
                             preferred_element_type=F32)
        ot = ot * (1.0 / denom)
        o = jnp.concatenate([ot[:, gg * w:(gg + 1) * w] for gg in range(g)], axis=0).T
        o_ref[r * w:(r + 1) * w, hh * g * d:(hh + 1) * g * d] = o.astype(o_ref.dtype)

    s_next = scores(*units[0])
    for n, (r, hh) in enumerate(units):
        s_cur = s_next
        if n + 1 < len(units):
            s_next = scores(*units[n + 1])
        finish(r, hh, s_cur)


def swa_attention(qkv, sinks, batch, seq_len, rows=512):
    t = qkv.shape[0]
    nq = seq_len // rows
    n_pairs = SWA_KV_HEADS // KV_PAIR
    k_blk0 = SWA_Q_WIDTH // V7X_LANES
    v_blk0 = (SWA_Q_WIDTH + SWA_KV_WIDTH) // V7X_LANES
    kern = functools.partial(_swa_kernel, rows=rows)
    return pl.pallas_call(
        kern,
        grid=(batch, n_pairs, nq),
        in_specs=[pl.BlockSpec(memory_space=pltpu.SMEM),
                  pl.BlockSpec((rows, Q_PAIR_WIDTH), lambda b, p, i: (b * nq + i, p)),
                  pl.BlockSpec((seq_len, V7X_LANES), lambda b, p, i: (b, k_blk0 + p)),
                  pl.BlockSpec((seq_len, V7X_LANES), lambda b, p, i: (b, v_blk0 + p))],
        out_specs=pl.BlockSpec((rows, Q_PAIR_WIDTH), lambda b, p, i: (b * nq + i, p)),
        out_shape=jax.ShapeDtypeStruct((t, SWA_Q_WIDTH), BF16),
        compiler_params=_compiler_params(("parallel", "parallel", "arbitrary")),
        name="swa_attention",
    )(sinks.astype(F32), qkv, qkv, qkv)


def _softmax_q_scale(head_dim, q_width, total_width):
    c = head_dim ** -0.5 * math.log2(math.e)
    return jnp.concatenate([jnp.full((q_width,), c, F32),
                            jnp.ones((total_width - q_width,), F32)])


def _lambda_init(layer_idx):
    return 0.8 - 0.6 * math.exp(-0.3 * layer_idx)


def kernel(x, norm_mix, norm_mlp, norm_final, w_in_even, w_pool, pool_scale, lambda_q1, lambda_k1, lambda_q2, lambda_k2, subln_g, w_out_even, w_in_odd, b_in_odd, sinks, w_out_odd, b_out_odd, w_up, w_down):
    b, s, d = x.shape
    x = x.reshape(b * s, d)
    w_in_even, w_out_even, w_in_odd, w_out_odd, w_up, w_down, w_pool = (
        w.astype(BF16) for w in (w_in_even, w_out_even, w_in_odd, w_out_odd, w_up, w_down, w_pool))
    diff_q_scale = _softmax_q_scale(DIFF_HEAD_DIM, DIFF_WIDTH, 3 * DIFF_WIDTH)
    swa_q_scale = _softmax_q_scale(SWA_HEAD_DIM, SWA_Q_WIDTH, SWA_Q_WIDTH + 2 * SWA_KV_WIDTH)
    xg, ssq = rmsnorm(x, norm_mix[0], BF16), None
    for i in range(DEPTH):
        j = i // 2
        if i % 2 == 0:
            u = matmul([xg], w_in_even, j, n=POOL_WIDTH, row_ssq=ssq, out_dtype=F32)
            qkv = matmul([xg], w_in_even, j, col0=POOL_WIDTH, row_ssq=ssq, col_scale=diff_q_scale)
            o_a = multiscale_pool(u, w_pool[j], pool_scale[j], s)
            o_b = diff_attention(qkv, lambda_q1[j], lambda_k1[j], lambda_q2[j], lambda_k2[j],
                                 subln_g[j], _lambda_init(i), b, s)
            x, xg, ssq = matmul([o_a, o_b], w_out_even, j, residual=x, next_gain=norm_mlp[i],
                                out_dtype=F32)
        else:
            qkv = matmul([xg], w_in_odd, j, row_ssq=ssq, bias=b_in_odd[j], col_scale=swa_q_scale)
            o_c = swa_attention(qkv, sinks[j], b, s)
            x, xg, ssq = matmul([o_c], w_out_odd, j, bias=b_out_odd[j], residual=x,
                                next_gain=norm_mlp[i], out_dtype=F32)
        a = matmul([xg], w_up, i, row_ssq=ssq, relu2=True)
        if i + 1 < DEPTH:
            x, xg, ssq = matmul([a], w_down, i, residual=x, next_gain=norm_mix[i + 1],
                                out_dtype=F32, bk=4096)
        else:
            x = matmul([a], w_down, i, residual=x, out_dtype=F32, bk=4096)
    return rmsnorm(x, norm_final, F32).reshape(b, s, d)
```

```python
import functools
import math

import jax
import jax.numpy as jnp
from jax import lax
from jax.experimental import pallas as pl
from jax.experimental.pallas import tpu as pltpu

D_MODEL = 4096
DEPTH = 4
POOL_WIDTH = D_MODEL // 2
POOL_WINDOWS = (2, 4, 8, 16)
POOL_GROUP = POOL_WIDTH // len(POOL_WINDOWS)
MAX_WINDOW = max(POOL_WINDOWS)
DIFF_HEAD_DIM = 64
DIFF_WIDTH = D_MODEL - POOL_WIDTH
DIFF_HEADS = DIFF_WIDTH // (2 * DIFF_HEAD_DIM)
SWA_HEADS = 64
SWA_KV_HEADS = 8
SWA_HEAD_DIM = D_MODEL // SWA_HEADS
SWA_GROUP = SWA_HEADS // SWA_KV_HEADS
SWA_WINDOW = 128
SWA_Q_WIDTH = SWA_HEADS * SWA_HEAD_DIM
SWA_KV_WIDTH = SWA_KV_HEADS * SWA_HEAD_DIM
D_FF = 4 * D_MODEL
RMS_EPS = 1e-5

V7X_LANES = 128
BF16_SUBLANES = 16
V7X_VMEM_BYTES = 64 * 1024 * 1024
VMEM_LIMIT_BYTES = V7X_VMEM_BYTES - 2 * 1024 * 1024

V7X_MXU_COLS = 256
ACC_STRIP = V7X_MXU_COLS

BF16 = jnp.bfloat16
F32 = jnp.float32
NEG_INF = float("-inf")


def _compiler_params(semantics):
    return pltpu.CompilerParams(dimension_semantics=semantics,
                                vmem_limit_bytes=VMEM_LIMIT_BYTES)


def _rmsnorm_kernel(x_ref, g_ref, o_ref):
    x = x_ref[...]
    ms = jnp.mean(x * x, axis=-1, keepdims=True)
    o_ref[...] = (x * lax.rsqrt(ms + RMS_EPS) * g_ref[...]).astype(o_ref.dtype)


def rmsnorm(x, g, out_dtype, block_rows=256):
    t, d = x.shape
    return pl.pallas_call(
        _rmsnorm_kernel,
        grid=(t // block_rows,),
        in_specs=[pl.BlockSpec((block_rows, d), lambda i: (i, 0)),
                  pl.BlockSpec((1, d), lambda i: (0, 0))],
        out_specs=pl.BlockSpec((block_rows, d), lambda i: (i, 0)),
        out_shape=jax.ShapeDtypeStruct((t, d), out_dtype),
        compiler_params=_compiler_params(("parallel",)),
        name="rmsnorm",
    )(x, g.reshape(1, d))


def _matmul_kernel(*refs, nk, n_parts, relu2, has_bias, has_scale, has_res, has_ssq, emit_norm,
                   norm_dim, n_casts):
    it = iter(refs)
    x_refs = [next(it) for _ in range(n_parts)]
    w_ref = next(it)
    q_ref = next(it) if has_ssq else None
    b_ref = next(it) if has_bias else None
    c_ref = next(it) if has_scale else None
    r_ref = next(it) if has_res else None
    g_ref = next(it) if emit_norm else None
    cast_srcs = [next(it) for _ in range(n_casts)]
    o_ref = next(it)
    og_ref, oq_ref = (next(it), next(it)) if emit_norm else (None, None)
    cast_dsts = [next(it) for _ in range(n_casts)]
    acc_ref = next(it) if nk > 1 else None

    for src_ref, dst_ref in zip(cast_srcs, cast_dsts):
        dst_ref[...] = src_ref[...].astype(dst_ref.dtype)

    def epilogue(acc):
        if has_ssq:
            ssq = q_ref[:, 0:V7X_LANES]
            for t in range(1, q_ref.shape[1] // V7X_LANES):
                ssq = ssq + q_ref[:, t * V7X_LANES:(t + 1) * V7X_LANES]
            rms = lax.rsqrt(ssq * (1.0 / norm_dim) + RMS_EPS)
            acc = jnp.concatenate(
                [acc[:, c:c + V7X_LANES] * rms for c in range(0, acc.shape[1], V7X_LANES)], axis=1)
        if has_bias:
            acc = acc + b_ref[...]
        if has_scale:
            acc = acc * c_ref[...]
        if relu2:
            a = jnp.maximum(acc, 0.0)
            acc = a * a
        if has_res:
            acc = acc + r_ref[...]
        o_ref[...] = acc.astype(o_ref.dtype)
        if emit_norm:
            og_ref[...] = (acc * g_ref[...]).astype(og_ref.dtype)
            oq_ref[...] = jnp.broadcast_to(jnp.sum(acc * acc, axis=1, keepdims=True), oq_ref.shape)

    if nk == 1:
        kp = w_ref.shape[0] // n_parts
        acc = jnp.dot(x_refs[0][...], w_ref[0:kp, :], preferred_element_type=F32)
        for p in range(1, n_parts):
            acc += jnp.dot(x_refs[p][...], w_ref[p * kp:(p + 1) * kp, :],
                           preferred_element_type=F32)
        epilogue(acc)
        return

    k = pl.program_id(2)

    @pl.when(k == 0)
    def _():
        acc_ref[...] = jnp.zeros_like(acc_ref)

    strip = min(ACC_STRIP, acc_ref.shape[1])
    for c0 in range(0, acc_ref.shape[1], strip):
        acc_ref[:, c0:c0 + strip] += jnp.dot(x_refs[0][...], w_ref[:, c0:c0 + strip],
                                             preferred_element_type=F32)

    @pl.when(k == nk - 1)
    def _():
        epilogue(acc_ref[...])


def matmul(x_parts, w_stack, layer, *, col0=0, n=None, row_ssq=None, bias=None, col_scale=None,
           residual=None, relu2=False, next_gain=None, cast_jobs=(), out_dtype=BF16,
           bm=1024, bn=1024, bk=None):
    x_parts = list(x_parts)
    m = x_parts[0].shape[0]
    kdim = sum(xp.shape[1] for xp in x_parts)
    n = w_stack.shape[2] - col0 if n is None else n
    bm, bn = math.gcd(bm, m), math.gcd(math.gcd(bn, n), col0) if col0 else math.gcd(bn, n)
    bk = kdim if bk is None else math.gcd(bk, kdim)
    nk = kdim // bk
    assert nk == 1 or len(x_parts) == 1
    jb0 = col0 // bn
    in_specs = [pl.BlockSpec((bm, bk // len(x_parts)), lambda i, j, k: (i, k))
                for _ in x_parts]
    in_specs.append(pl.BlockSpec((None, bk, bn), lambda i, j, k: (layer, k, jb0 + j)))
    args = x_parts + [w_stack]
    if row_ssq is not None:
        in_specs.append(pl.BlockSpec((bm, row_ssq.shape[1]), lambda i, j, k: (i, 0)))
        args.append(row_ssq)
    if bias is not None:
        in_specs.append(pl.BlockSpec((1, bn), lambda i, j, k: (0, j)))
        args.append(bias.reshape(1, n).astype(F32))
    if col_scale is not None:
        in_specs.append(pl.BlockSpec((1, bn), lambda i, j, k: (0, j)))
        args.append(col_scale.reshape(1, n).astype(F32))
    if residual is not None:
        in_specs.append(pl.BlockSpec((bm, bn), lambda i, j, k: (i, j)))
        args.append(residual)
    out_spec = pl.BlockSpec((bm, bn), lambda i, j, k: (i, j))
    out_specs, out_shape = [out_spec], [jax.ShapeDtypeStruct((m, n), out_dtype)]
    if next_gain is not None:
        in_specs.append(pl.BlockSpec((1, bn), lambda i, j, k: (0, j)))
        args.append(next_gain.reshape(1, n).astype(F32))
        out_specs += [out_spec, pl.BlockSpec((bm, V7X_LANES), lambda i, j, k: (i, j))]
        out_shape += [jax.ShapeDtypeStruct((m, n), BF16),
                      jax.ShapeDtypeStruct((m, n // bn * V7X_LANES), F32)]
    nj = n // bn
    n_steps = (m // bm) * nj * nk
    for src, src_layer in cast_jobs:
        rows, cols = src.shape[1] // n_steps, src.shape[2]
        assert rows * n_steps == src.shape[1] and rows % BF16_SUBLANES == 0, src.shape
        in_specs.append(pl.BlockSpec(
            (None, rows, cols), lambda i, j, k, l=src_layer: (l, (i * nj + j) * nk + k, 0)))
        args.append(src)
        out_specs.append(pl.BlockSpec(
            (None, rows, cols), lambda i, j, k: (0, (i * nj + j) * nk + k, 0)))
        out_shape.append(jax.ShapeDtypeStruct((1,) + src.shape[1:], BF16))
    kern = functools.partial(_matmul_kernel, nk=nk, n_parts=len(x_parts), relu2=relu2,
                             has_bias=bias is not None, has_scale=col_scale is not None,
                             has_res=residual is not None, has_ssq=row_ssq is not None,
                             emit_norm=next_gain is not None, norm_dim=kdim,
                             n_casts=len(cast_jobs))
    outs = pl.pallas_call(
        kern,
        grid=(m // bm, n // bn, nk),
        in_specs=in_specs,
        out_specs=out_specs,
        out_shape=out_shape,
        scratch_shapes=[pltpu.VMEM((bm, bn), F32)] if nk > 1 else [],
        compiler_params=_compiler_params(("parallel", "parallel", "arbitrary")),
        name="matmul",
    )(*args)
    return outs[0] if len(outs) == 1 else tuple(outs)


def _pool_kernel(u_ref, halo_ref, w_ref, s_ref, o_ref, *, rows, blocks_per_seq):
    i = pl.program_id(0)
    seq_blk = i % blocks_per_seq
    halo = jnp.where(seq_blk == 0, 0.0, halo_ref[...])
    t = seq_blk * rows + lax.broadcasted_iota(jnp.int32, (rows, 1), 0)
    for g, win in enumerate(POOL_WINDOWS):
        lo, hi = g * POOL_GROUP, (g + 1) * POOL_GROUP
        cur = u_ref[:, lo:hi]
        ext = jnp.concatenate([halo[:, lo:hi], cur], axis=0)
        win_sum = cur
        for j in range(1, win):
            win_sum = win_sum + ext[MAX_WINDOW - j:MAX_WINDOW - j + rows]
        count = jnp.minimum(t + 1, win).astype(F32)
        p = (win_sum / count - cur).astype(BF16)
        y = jnp.dot(p, w_ref[g], preferred_element_type=F32)
        o_ref[:, lo:hi] = (y * s_ref[:, lo:hi]).astype(o_ref.dtype)


def multiscale_pool(u, w_pool, pool_scale, seq_len, rows=512):
    t, c = u.shape
    halo_per_blk = rows // MAX_WINDOW
    kern = functools.partial(_pool_kernel, rows=rows, blocks_per_seq=seq_len // rows)
    return pl.pallas_call(
        kern,
        grid=(t // rows,),
        in_specs=[pl.BlockSpec((rows, c), lambda i: (i, 0)),
                  pl.BlockSpec((MAX_WINDOW, c),
                               lambda i: (jnp.maximum(i * halo_per_blk - 1, 0), 0)),
                  pl.BlockSpec((len(POOL_WINDOWS), POOL_GROUP, POOL_GROUP),
                               lambda i: (0, 0, 0)),
                  pl.BlockSpec((1, c), lambda i: (0, 0))],
        out_specs=pl.BlockSpec((rows, c), lambda i: (i, 0)),
        out_shape=jax.ShapeDtypeStruct((t, c), BF16),
        compiler_params=_compiler_params(("parallel",)),
        name="multiscale_pool",
    )(u, u, w_pool, pool_scale.reshape(1, c))


def _diff_attn_kernel(q_ref, k_ref, v_ref, lq1_ref, lk1_ref, lq2_ref, lk2_ref, g_ref,
                      o_ref, sa_scr, sb_scr, *, bq, bk, lam_init):
    i = pl.program_id(2)
    d = DIFF_HEAD_DIM
    qt = q_ref[...].astype(F32).T.astype(BF16)
    feat = lax.broadcasted_iota(jnp.int32, qt.shape, 0)
    zero = jnp.zeros_like(qt)
    qt_maps = (jnp.where(feat < d, qt, zero), jnp.where(feat >= d, qt, zero))

    def scores(j, s_scr, c, q0=0):
        kb = k_ref[pl.ds(pl.multiple_of(j * bk, bk), bk), :]
        s_scr[c, :, q0:] = jnp.dot(kb, qt_maps[c][:, q0:], preferred_element_type=F32)

    def softmax_pv(s, vb, m_prev, l_prev, acc_prev, causal):
        if causal:
            key = lax.broadcasted_iota(jnp.int32, s.shape, 0)
            qry = lax.broadcasted_iota(jnp.int32, s.shape, 1)
            s = jnp.where(key <= qry, s, NEG_INF)
        m_new = jnp.maximum(m_prev, jnp.max(s, axis=0, keepdims=True))
        alpha = jnp.exp2(m_prev - m_new)
        p = jnp.exp2(s - m_new)
        l_new = alpha * l_prev + jnp.sum(p, axis=0, keepdims=True)
        pv = lax.dot_general(vb, p.astype(BF16), (((0,), (0,)), ((), ())),
                             preferred_element_type=F32)
        return m_new, l_new, alpha * acc_prev + pv

    def update(j, s_scr, c, stats_c, halves=None):
        vb = v_ref[pl.ds(pl.multiple_of(j * bk, bk), bk), :]
        if halves is None:
            return softmax_pv(s_scr[c], vb, *stats_c, False)
        parts = []
        for h, mode in enumerate(halves):
            sl = slice(h * bk, (h + 1) * bk)
            st = tuple(a[:, sl] for a in stats_c)
            parts.append(st if mode == "skip" else
                         softmax_pv(s_scr[c, :, sl], vb, *st, mode == "causal"))
        return tuple(jnp.concatenate(a, axis=1) for a in zip(*parts))

    def step(j, cur_scr, nxt_scr, stats, halves=None, nxt_q0=0, prefetch=True):
        out = []
        for c in range(2):
            if prefetch:
                scores(j + 1, nxt_scr, c, nxt_q0)
            out.append(update(j, cur_scr, c, stats[c], halves))
        return tuple(out)

    init_one = (jnp.full((1, bq), NEG_INF, F32), jnp.zeros((1, bq), F32),
                jnp.zeros((2 * d, bq), F32))

    for c in range(2):
        scores(0, sa_scr, c)

    def pair(jj, stats):
        stats = step(2 * jj, sa_scr, sb_scr, stats)
        return step(2 * jj + 1, sb_scr, sa_scr, stats)

    stats = lax.fori_loop(0, i, pair, (init_one, init_one))
    stats = step(2 * i, sa_scr, sb_scr, stats, halves=("causal", None), nxt_q0=bk)
    (_, l1, acc1), (_, l2, acc2) = step(2 * i + 1, sb_scr, sa_scr, stats,
                                        halves=("skip", "causal"), prefetch=False)

    lam = (jnp.exp(jnp.sum(lq1_ref[...] * lk1_ref[...], axis=-1, keepdims=True))
           - jnp.exp(jnp.sum(lq2_ref[...] * lk2_ref[...], axis=-1, keepdims=True))
           + lam_init)
    o = (acc1 / l1 - lam * (acc2 / l2)).T
    ms = jnp.mean(o * o, axis=-1, keepdims=True)
    o = o * lax.rsqrt(ms + RMS_EPS) * g_ref[...] * (1.0 - lam_init)
    o_ref[...] = o.astype(o_ref.dtype)


def diff_attention(qkv, lq1, lk1, lq2, lk2, subln_g, lam_init, batch, seq_len, bk=512):
    t = qkv.shape[0]
    bq = 2 * bk
    nq = seq_len // bq
    hw = 2 * DIFF_HEAD_DIM
    vec = lambda a: a.reshape(1, -1).astype(F32)
    vec_spec = lambda n: pl.BlockSpec((1, n), lambda b, h, i: (0, 0))
    kern = functools.partial(_diff_attn_kernel, bq=bq, bk=bk, lam_init=lam_init)
    return pl.pallas_call(
        kern,
        grid=(batch, DIFF_HEADS, nq),
        in_specs=[pl.BlockSpec((bq, hw), lambda b, h, i: (b * nq + i, h)),
                  pl.BlockSpec((seq_len, hw), lambda b, h, i: (b, DIFF_HEADS + h)),
                  pl.BlockSpec((seq_len, hw), lambda b, h, i: (b, 2 * DIFF_HEADS + h)),
                  vec_spec(DIFF_HEAD_DIM), vec_spec(DIFF_HEAD_DIM),
                  vec_spec(DIFF_HEAD_DIM), vec_spec(DIFF_HEAD_DIM), vec_spec(hw)],
        out_specs=pl.BlockSpec((bq, hw), lambda b, h, i: (b * nq + i, h)),
        out_shape=jax.ShapeDtypeStruct((t, DIFF_WIDTH), BF16),
        scratch_shapes=[pltpu.VMEM((2, bk, bq), F32)] * 2,
        compiler_params=_compiler_params(("parallel", "parallel", "arbitrary")),
        name="diff_attention",
    )(qkv, qkv, qkv, vec(lq1), vec(lk1), vec(lq2), vec(lk2), vec(subln_g))


KV_PAIR = V7X_LANES // SWA_HEAD_DIM
Q_PAIR_WIDTH = KV_PAIR * SWA_GROUP * SWA_HEAD_DIM


def _swa_kernel(sink_ref, q_ref, k_ref, v_ref, o_ref, *, rows):
    p_idx = pl.program_id(1)
    i = pl.program_id(2)
    w, d, g = SWA_WINDOW, SWA_HEAD_DIM, SWA_GROUP
    log2e = math.log2(math.e)
    units = [(r, hh) for r in range(rows // w) for hh in range(KV_PAIR)]

    def positions(r):
        base = i * rows + r * w
        return base, pl.multiple_of(jnp.maximum(base - w, 0), w)

    def scores(r, hh):
        _, start = positions(r)
        kk = k_ref[pl.ds(start, 2 * w), :]
        q = q_ref[r * w:(r + 1) * w, hh * g * d:(hh + 1) * g * d]
        qt = q.astype(F32).T.astype(BF16)
        zeros = jnp.zeros((d, w), BF16)
        cols = [jnp.concatenate([qt[gg * d:(gg + 1) * d], zeros] if hh == 0
                                else [zeros, qt[gg * d:(gg + 1) * d]], axis=0)
                for gg in range(g)]
        return jnp.dot(kk, jnp.concatenate(cols, axis=1), preferred_element_type=F32)

    def finish(r, hh, s):
        base, start = positions(r)
        k_pos = start + lax.broadcasted_iota(jnp.int32, (2 * w, w), 0)
        q_pos = base + lax.broadcasted_iota(jnp.int32, (2 * w, w), 1)
        mask = (k_pos <= q_pos) & (q_pos - k_pos < w)
        s = jnp.where(jnp.concatenate([mask] * g, axis=1), s, NEG_INF)
        head0 = (p_idx * KV_PAIR + hh) * g
        sink = jnp.concatenate([jnp.full((1, w), sink_ref[head0 + gg] * log2e, F32)
                                for gg in range(g)], axis=1)
        m = jnp.maximum(jnp.max(s, axis=0, keepdims=True), sink)
        e = jnp.exp2(s - m)
        denom = jnp.sum(e, axis=0, keepdims=True) + jnp.exp2(sink - m)
        vv = v_ref[pl.ds(start, 2 * w), :][:, hh * d:(hh + 1) * d]
        ot = lax.dot_general(vv, e.astype(BF16), (((0,), (0,)), ((), ())),
                             preferred_element_type=F32)
        ot = ot * (1.0 / denom)
        o = jnp.concatenate([ot[:, gg * w:(gg + 1) * w] for gg in range(g)], axis=0).T
        o_ref[r * w:(r + 1) * w, hh * g * d:(hh + 1) * g * d] = o.astype(o_ref.dtype)

    s_next = scores(*units[0])
    for n, (r, hh) in enumerate(units):
        s_cur = s_next
        if n + 1 < len(units):
            s_next = scores(*units[n + 1])
        finish(r, hh, s_cur)


def swa_attention(qkv, sinks, batch, seq_len, rows=512):
    t = qkv.shape[0]
    nq = seq_len // rows
    n_pairs = SWA_KV_HEADS // KV_PAIR
    k_blk0 = SWA_Q_WIDTH // V7X_LANES
    v_blk0 = (SWA_Q_WIDTH + SWA_KV_WIDTH) // V7X_LANES
    kern = functools.partial(_swa_kernel, rows=rows)
    return pl.pallas_call(
        kern,
        grid=(batch, n_pairs, nq),
        in_specs=[pl.BlockSpec(memory_space=pltpu.SMEM),
                  pl.BlockSpec((rows, Q_PAIR_WIDTH), lambda b, p, i: (b * nq + i, p)),
                  pl.BlockSpec((seq_len, V7X_LANES), lambda b, p, i: (b, k_blk0 + p)),
                  pl.BlockSpec((seq_len, V7X_LANES), lambda b, p, i: (b, v_blk0 + p))],
        out_specs=pl.BlockSpec((rows, Q_PAIR_WIDTH), lambda b, p, i: (b * nq + i, p)),
        out_shape=jax.ShapeDtypeStruct((t, SWA_Q_WIDTH), BF16),
        compiler_params=_compiler_params(("parallel", "parallel", "arbitrary")),
        name="swa_attention",
    )(sinks.astype(F32), qkv, qkv, qkv)


def _softmax_q_scale(head_dim, q_width, total_width):
    c = head_dim ** -0.5 * math.log2(math.e)
    return jnp.concatenate([jnp.full((q_width,), c, F32),
                            jnp.ones((total_width - q_width,), F32)])


def _lambda_init(layer_idx):
    return 0.8 - 0.6 * math.exp(-0.3 * layer_idx)


def kernel(x, norm_mix, norm_mlp, norm_final, w_in_even, w_pool, pool_scale, lambda_q1, lambda_k1, lambda_q2, lambda_k2, subln_g, w_out_even, w_in_odd, b_in_odd, sinks, w_out_odd, b_out_odd, w_up, w_down):
    b, s, d = x.shape
    x = x.reshape(b * s, d)
    diff_q_scale = _softmax_q_scale(DIFF_HEAD_DIM, DIFF_WIDTH, 3 * DIFF_WIDTH)
    swa_q_scale = _softmax_q_scale(SWA_HEAD_DIM, SWA_Q_WIDTH, SWA_Q_WIDTH + 2 * SWA_KV_WIDTH)
    w_pool = w_pool.astype(BF16)

    def mixer_weights(layer):
        return (w_in_even, w_out_even) if layer % 2 == 0 else (w_in_odd, w_out_odd)

    w_in, w_out = (w[0:1].astype(BF16) for w in mixer_weights(0))
    w_up_b = w_up[0:1].astype(BF16)
    xg, ssq = rmsnorm(x, norm_mix[0], BF16), None
    for i in range(DEPTH):
        j = i // 2
        if i % 2 == 0:
            u = matmul([xg], w_in, 0, n=POOL_WIDTH, row_ssq=ssq, out_dtype=F32)
            qkv = matmul([xg], w_in, 0, col0=POOL_WIDTH, row_ssq=ssq, col_scale=diff_q_scale)
            o_a = multiscale_pool(u, w_pool[j], pool_scale[j], s)
            o_b = diff_attention(qkv, lambda_q1[j], lambda_k1[j], lambda_q2[j], lambda_k2[j],
                                 subln_g[j], _lambda_init(i), b, s)
            x, xg, ssq = matmul([o_a, o_b], w_out, 0, residual=x, next_gain=norm_mlp[i],
                                out_dtype=F32)
        else:
            qkv = matmul([xg], w_in, 0, row_ssq=ssq, bias=b_in_odd[j], col_scale=swa_q_scale)
            o_c = swa_attention(qkv, sinks[j], b, s)
            x, xg, ssq = matmul([o_c], w_out, 0, bias=b_out_odd[j], residual=x,
                                next_gain=norm_mlp[i], out_dtype=F32)
        jobs = [(w_down, i)]
        if i + 1 < DEPTH:
            nxt_in, nxt_out = mixer_weights(i + 1)
            jobs += [(w_up, i + 1), (nxt_in, (i + 1) // 2), (nxt_out, (i + 1) // 2)]
        a, w_down_b, *nxt = matmul([xg], w_up_b, 0, row_ssq=ssq, relu2=True, cast_jobs=jobs)
        if i + 1 < DEPTH:
            x, xg, ssq = matmul([a], w_down_b, 0, residual=x, next_gain=norm_mix[i + 1],
                                out_dtype=F32, bk=4096)
            w_up_b, w_in, w_out = nxt
        else:
            x = matmul([a], w_down_b, 0, residual=x, out_dtype=F32, bk=4096)
    return rmsnorm(x, norm_final, F32).reshape(b, s, d)
```

```python
import functools
import math

import jax
import jax.numpy as jnp
from jax import lax
from jax.experimental import pallas as pl
from jax.experimental.pallas import tpu as pltpu

D_MODEL = 4096
DEPTH = 4
POOL_WIDTH = D_MODEL // 2
POOL_WINDOWS = (2, 4, 8, 16)
POOL_GROUP = POOL_WIDTH // len(POOL_WINDOWS)
MAX_WINDOW = max(POOL_WINDOWS)
DIFF_HEAD_DIM = 64
DIFF_WIDTH = D_MODEL - POOL_WIDTH
DIFF_HEADS = DIFF_WIDTH // (2 * DIFF_HEAD_DIM)
SWA_HEADS = 64
SWA_KV_HEADS = 8
SWA_HEAD_DIM = D_MODEL // SWA_HEADS
SWA_GROUP = SWA_HEADS // SWA_KV_HEADS
SWA_WINDOW = 128
SWA_Q_WIDTH = SWA_HEADS * SWA_HEAD_DIM
SWA_KV_WIDTH = SWA_KV_HEADS * SWA_HEAD_DIM
D_FF = 4 * D_MODEL
RMS_EPS = 1e-5

V7X_LANES = 128
BF16_SUBLANES = 16
V7X_VMEM_BYTES = 64 * 1024 * 1024
VMEM_LIMIT_BYTES = V7X_VMEM_BYTES - 2 * 1024 * 1024

V7X_MXU_COLS = 256
ACC_STRIP = V7X_MXU_COLS

BF16 = jnp.bfloat16
F32 = jnp.float32
NEG_INF = float("-inf")


def _compiler_params(semantics):
    return pltpu.CompilerParams(dimension_semantics=semantics,
                                vmem_limit_bytes=VMEM_LIMIT_BYTES)


def _rmsnorm_kernel(x_ref, g_ref, o_ref):
    x = x_ref[...]
    ms = jnp.mean(x * x, axis=-1, keepdims=True)
    o_ref[...] = (x * lax.rsqrt(ms + RMS_EPS) * g_ref[...]).astype(o_ref.dtype)


def rmsnorm(x, g, out_dtype, block_rows=256):
    t, d = x.shape
    return pl.pallas_call(
        _rmsnorm_kernel,
        grid=(t // block_rows,),
        in_specs=[pl.BlockSpec((block_rows, d), lambda i: (i, 0)),
                  pl.BlockSpec((1, d), lambda i: (0, 0))],
        out_specs=pl.BlockSpec((block_rows, d), lambda i: (i, 0)),
        out_shape=jax.ShapeDtypeStruct((t, d), out_dtype),
        compiler_params=_compiler_params(("parallel",)),
        name="rmsnorm",
    )(x, g.reshape(1, d))


def _matmul_kernel(*refs, nk, n_parts, relu2, has_bias, has_scale, has_res, has_ssq, emit_norm,
                   norm_dim, n_casts, n_cast_slabs):
    it = iter(refs)
    x_refs = [next(it) for _ in range(n_parts)]
    w_ref = next(it)
    q_ref = next(it) if has_ssq else None
    b_ref = next(it) if has_bias else None
    c_ref = next(it) if has_scale else None
    r_ref = next(it) if has_res else None
    g_ref = next(it) if emit_norm else None
    cast_srcs = [next(it) for _ in range(n_casts)]
    o_ref = next(it)
    og_ref, oq_ref = (next(it), next(it)) if emit_norm else (None, None)
    cast_dsts = [next(it) for _ in range(n_casts)]
    acc_ref = next(it) if nk > 1 else None

    def cast_slabs():
        for src_ref, dst_ref in zip(cast_srcs, cast_dsts):
            dst_ref[...] = src_ref[...].astype(dst_ref.dtype)

    if n_cast_slabs is None:
        cast_slabs()
    else:
        step_id = ((pl.program_id(0) * pl.num_programs(1) + pl.program_id(1)) * pl.num_programs(2)
                   + pl.program_id(2))
        pl.when(step_id < n_cast_slabs)(cast_slabs)

    def epilogue(acc):
        if has_ssq:
            ssq = q_ref[:, 0:V7X_LANES]
            for t in range(1, q_ref.shape[1] // V7X_LANES):
                ssq = ssq + q_ref[:, t * V7X_LANES:(t + 1) * V7X_LANES]
            rms = lax.rsqrt(ssq * (1.0 / norm_dim) + RMS_EPS)
            acc = jnp.concatenate(
                [acc[:, c:c + V7X_LANES] * rms for c in range(0, acc.shape[1], V7X_LANES)], axis=1)
        if has_bias:
            acc = acc + b_ref[...]
        if has_scale:
            acc = acc * c_ref[...]
        if relu2:
            a = jnp.maximum(acc, 0.0)
            acc = a * a
        if has_res:
            acc = acc + r_ref[...]
        o_ref[...] = acc.astype(o_ref.dtype)
        if emit_norm:
            og_ref[...] = (acc * g_ref[...]).astype(og_ref.dtype)
            oq_ref[...] = jnp.broadcast_to(jnp.sum(acc * acc, axis=1, keepdims=True), oq_ref.shape)

    if nk == 1:
        kp = w_ref.shape[0] // n_parts
        acc = jnp.dot(x_refs[0][...], w_ref[0:kp, :], preferred_element_type=F32)
        for p in range(1, n_parts):
            acc += jnp.dot(x_refs[p][...], w_ref[p * kp:(p + 1) * kp, :],
                           preferred_element_type=F32)
        epilogue(acc)
        return

    k = pl.program_id(2)

    @pl.when(k == 0)
    def _():
        acc_ref[...] = jnp.zeros_like(acc_ref)

    strip = min(ACC_STRIP, acc_ref.shape[1])
    for c0 in range(0, acc_ref.shape[1], strip):
        acc_ref[:, c0:c0 + strip] += jnp.dot(x_refs[0][...], w_ref[:, c0:c0 + strip],
                                             preferred_element_type=F32)

    @pl.when(k == nk - 1)
    def _():
        epilogue(acc_ref[...])


def matmul(x_parts, w_stack, layer, *, col0=0, n=None, row_ssq=None, bias=None, col_scale=None,
           residual=None, relu2=False, next_gain=None, cast_jobs=(), out_dtype=BF16,
           bm=1024, bn=1024, bk=None):
    x_parts = list(x_parts)
    m = x_parts[0].shape[0]
    kdim = sum(xp.shape[1] for xp in x_parts)
    n = w_stack.shape[2] - col0 if n is None else n
    bm, bn = math.gcd(bm, m), math.gcd(math.gcd(bn, n), col0) if col0 else math.gcd(bn, n)
    bk = kdim if bk is None else math.gcd(bk, kdim)
    nk = kdim // bk
    assert nk == 1 or len(x_parts) == 1
    jb0 = col0 // bn
    in_specs = [pl.BlockSpec((bm, bk // len(x_parts)), lambda i, j, k: (i, k))
                for _ in x_parts]
    in_specs.append(pl.BlockSpec((None, bk, bn), lambda i, j, k: (layer, k, jb0 + j)))
    args = x_parts + [w_stack]
    if row_ssq is not None:
        in_specs.append(pl.BlockSpec((bm, row_ssq.shape[1]), lambda i, j, k: (i, 0)))
        args.append(row_ssq)
    if bias is not None:
        in_specs.append(pl.BlockSpec((1, bn), lambda i, j, k: (0, j)))
        args.append(bias.reshape(1, n).astype(F32))
    if col_scale is not None:
        in_specs.append(pl.BlockSpec((1, bn), lambda i, j, k: (0, j)))
        args.append(col_scale.reshape(1, n).astype(F32))
    if residual is not None:
        in_specs.append(pl.BlockSpec((bm, bn), lambda i, j, k: (i, j)))
        args.append(residual)
    out_spec = pl.BlockSpec((bm, bn), lambda i, j, k: (i, j))
    out_specs, out_shape = [out_spec], [jax.ShapeDtypeStruct((m, n), out_dtype)]
    if next_gain is not None:
        in_specs.append(pl.BlockSpec((1, bn), lambda i, j, k: (0, j)))
        args.append(next_gain.reshape(1, n).astype(F32))
        out_specs += [out_spec, pl.BlockSpec((bm, V7X_LANES), lambda i, j, k: (i, j))]
        out_shape += [jax.ShapeDtypeStruct((m, n), BF16),
                      jax.ShapeDtypeStruct((m, n // bn * V7X_LANES), F32)]
    nj = n // bn
    n_steps = (m // bm) * nj * nk
    n_slabs = 1 << (n_steps.bit_length() - 1)

    def slab(i, j, k):
        return jnp.minimum((i * nj + j) * nk + k, n_slabs - 1)

    for src, src_layer in cast_jobs:
        rows, cols = src.shape[1] // n_slabs, src.shape[2]
        assert rows * n_slabs == src.shape[1] and rows % BF16_SUBLANES == 0, src.shape
        in_specs.append(pl.BlockSpec(
            (None, rows, cols), lambda i, j, k, l=src_layer: (l, slab(i, j, k), 0)))
        args.append(src)
        out_specs.append(pl.BlockSpec((None, rows, cols), lambda i, j, k: (0, slab(i, j, k), 0)))
        out_shape.append(jax.ShapeDtypeStruct((1,) + src.shape[1:], BF16))
    kern = functools.partial(_matmul_kernel, nk=nk, n_parts=len(x_parts), relu2=relu2,
                             has_bias=bias is not None, has_scale=col_scale is not None,
                             has_res=residual is not None, has_ssq=row_ssq is not None,
                             emit_norm=next_gain is not None, norm_dim=kdim,
                             n_casts=len(cast_jobs),
                             n_cast_slabs=None if n_slabs == n_steps else n_slabs)
    outs = pl.pallas_call(
        kern,
        grid=(m // bm, n // bn, nk),
        in_specs=in_specs,
        out_specs=out_specs,
        out_shape=out_shape,
        scratch_shapes=[pltpu.VMEM((bm, bn), F32)] if nk > 1 else [],
        compiler_params=_compiler_params(("parallel", "parallel", "arbitrary")),
        name="matmul",
    )(*args)
    return outs[0] if len(outs) == 1 else tuple(outs)


def _pool_kernel(u_ref, halo_ref, w_ref, s_ref, o_ref, *, rows, blocks_per_seq):
    i = pl.program_id(0)
    seq_blk = i % blocks_per_seq
    halo = jnp.where(seq_blk == 0, 0.0, halo_ref[...])
    t = seq_blk * rows + lax.broadcasted_iota(jnp.int32, (rows, 1), 0)
    for g, win in enumerate(POOL_WINDOWS):
        lo, hi = g * POOL_GROUP, (g + 1) * POOL_GROUP
        cur = u_ref[:, lo:hi]
        ext = jnp.concatenate([halo[:, lo:hi], cur], axis=0)
        win_sum = cur
        for j in range(1, win):
            win_sum = win_sum + ext[MAX_WINDOW - j:MAX_WINDOW - j + rows]
        count = jnp.minimum(t + 1, win).astype(F32)
        p = (win_sum / count - cur).astype(BF16)
        y = jnp.dot(p, w_ref[g], preferred_element_type=F32)
        o_ref[:, lo:hi] = (y * s_ref[:, lo:hi]).astype(o_ref.dtype)


def multiscale_pool(u, w_pool, pool_scale, seq_len, rows=512):
    t, c = u.shape
    halo_per_blk = rows // MAX_WINDOW
    kern = functools.partial(_pool_kernel, rows=rows, blocks_per_seq=seq_len // rows)
    return pl.pallas_call(
        kern,
        grid=(t // rows,),
        in_specs=[pl.BlockSpec((rows, c), lambda i: (i, 0)),
                  pl.BlockSpec((MAX_WINDOW, c),
                               lambda i: (jnp.maximum(i * halo_per_blk - 1, 0), 0)),
                  pl.BlockSpec((len(POOL_WINDOWS), POOL_GROUP, POOL_GROUP),
                               lambda i: (0, 0, 0)),
                  pl.BlockSpec((1, c), lambda i: (0, 0))],
        out_specs=pl.BlockSpec((rows, c), lambda i: (i, 0)),
        out_shape=jax.ShapeDtypeStruct((t, c), BF16),
        compiler_params=_compiler_params(("parallel",)),
        name="multiscale_pool",
    )(u, u, w_pool, pool_scale.reshape(1, c))


def _diff_attn_kernel(q_ref, k_ref, v_ref, lq1_ref, lk1_ref, lq2_ref, lk2_ref, g_ref,
                      o_ref, sa_scr, sb_scr, *, bq, bk, lam_init):
    i = pl.program_id(2)
    d = DIFF_HEAD_DIM
    qt = q_ref[...].astype(F32).T.astype(BF16)
    feat = lax.broadcasted_iota(jnp.int32, qt.shape, 0)
    zero = jnp.zeros_like(qt)
    qt_maps = (jnp.where(feat < d, qt, zero), jnp.where(feat >= d, qt, zero))

    def scores(j, s_scr, c, q0=0):
        kb = k_ref[pl.ds(pl.multiple_of(j * bk, bk), bk), :]
        s_scr[c, :, q0:] = jnp.dot(kb, qt_maps[c][:, q0:], preferred_element_type=F32)

    def softmax_pv(s, vb, m_prev, l_prev, acc_prev, causal):
        if causal:
            key = lax.broadcasted_iota(jnp.int32, s.shape, 0)
            qry = lax.broadcasted_iota(jnp.int32, s.shape, 1)
            s = jnp.where(key <= qry, s, NEG_INF)
        m_new = jnp.maximum(m_prev, jnp.max(s, axis=0, keepdims=True))
        alpha = jnp.exp2(m_prev - m_new)
        p = jnp.exp2(s - m_new)
        l_new = alpha * l_prev + jnp.sum(p, axis=0, keepdims=True)
        pv = lax.dot_general(vb, p.astype(BF16), (((0,), (0,)), ((), ())),
                             preferred_element_type=F32)
        return m_new, l_new, alpha * acc_prev + pv

    def update(j, s_scr, c, stats_c, halves=None):
        vb = v_ref[pl.ds(pl.multiple_of(j * bk, bk), bk), :]
        if halves is None:
            return softmax_pv(s_scr[c], vb, *stats_c, False)
        parts = []
        for h, mode in enumerate(halves):
            sl = slice(h * bk, (h + 1) * bk)
            st = tuple(a[:, sl] for a in stats_c)
            parts.append(st if mode == "skip" else
                         softmax_pv(s_scr[c, :, sl], vb, *st, mode == "causal"))
        return tuple(jnp.concatenate(a, axis=1) for a in zip(*parts))

    def step(j, cur_scr, nxt_scr, stats, halves=None, nxt_q0=0, prefetch=True):
        out = []
        for c in range(2):
            if prefetch:
                scores(j + 1, nxt_scr, c, nxt_q0)
            out.append(update(j, cur_scr, c, stats[c], halves))
        return tuple(out)

    init_one = (jnp.full((1, bq), NEG_INF, F32), jnp.zeros((1, bq), F32),
                jnp.zeros((2 * d, bq), F32))

    for c in range(2):
        scores(0, sa_scr, c)

    def pair(jj, stats):
        stats = step(2 * jj, sa_scr, sb_scr, stats)
        return step(2 * jj + 1, sb_scr, sa_scr, stats)

    stats = lax.fori_loop(0, i, pair, (init_one, init_one))
    stats = step(2 * i, sa_scr, sb_scr, stats, halves=("causal", None), nxt_q0=bk)
    (_, l1, acc1), (_, l2, acc2) = step(2 * i + 1, sb_scr, sa_scr, stats,
                                        halves=("skip", "causal"), prefetch=False)

    lam = (jnp.exp(jnp.sum(lq1_ref[...] * lk1_ref[...], axis=-1, keepdims=True))
           - jnp.exp(jnp.sum(lq2_ref[...] * lk2_ref[...], axis=-1, keepdims=True))
           + lam_init)
    o = (acc1 / l1 - lam * (acc2 / l2)).T
    ms = jnp.mean(o * o, axis=-1, keepdims=True)
    o = o * lax.rsqrt(ms + RMS_EPS) * g_ref[...] * (1.0 - lam_init)
    o_ref[...] = o.astype(o_ref.dtype)


def diff_attention(qkv, lq1, lk1, lq2, lk2, subln_g, lam_init, batch, seq_len, bk=512):
    t = qkv.shape[0]
    bq = 2 * bk
    nq = seq_len // bq
    hw = 2 * DIFF_HEAD_DIM
    vec = lambda a: a.reshape(1, -1).astype(F32)
    vec_spec = lambda n: pl.BlockSpec((1, n), lambda b, h, i: (0, 0))
    kern = functools.partial(_diff_attn_kernel, bq=bq, bk=bk, lam_init=lam_init)
    return pl.pallas_call(
        kern,
        grid=(batch, DIFF_HEADS, nq),
        in_specs=[pl.BlockSpec((bq, hw), lambda b, h, i: (b * nq + i, h)),
                  pl.BlockSpec((seq_len, hw), lambda b, h, i: (b, DIFF_HEADS + h)),
                  pl.BlockSpec((seq_len, hw), lambda b, h, i: (b, 2 * DIFF_HEADS + h)),
                  vec_spec(DIFF_HEAD_DIM), vec_spec(DIFF_HEAD_DIM),
                  vec_spec(DIFF_HEAD_DIM), vec_spec(DIFF_HEAD_DIM), vec_spec(hw)],
        out_specs=pl.BlockSpec((bq, hw), lambda b, h, i: (b * nq + i, h)),
        out_shape=jax.ShapeDtypeStruct((t, DIFF_WIDTH), BF16),
        scratch_shapes=[pltpu.VMEM((2, bk, bq), F32)] * 2,
        compiler_params=_compiler_params(("parallel", "parallel", "arbitrary")),
        name="diff_attention",
    )(qkv, qkv, qkv, vec(lq1), vec(lk1), vec(lq2), vec(lk2), vec(subln_g))


KV_PAIR = V7X_LANES // SWA_HEAD_DIM
Q_PAIR_WIDTH = KV_PAIR * SWA_GROUP * SWA_HEAD_DIM


def _swa_kernel(sink_ref, q_ref, k_ref, v_ref, o_ref, *, rows):
    p_idx = pl.program_id(1)
    i = pl.program_id(2)
    w, d, g = SWA_WINDOW, SWA_HEAD_DIM, SWA_GROUP
    log2e = math.log2(math.e)
    units = [(r, hh) for r in range(rows // w) for hh in range(KV_PAIR)]

    def positions(r):
        base = i * rows + r * w
        return base, pl.multiple_of(jnp.maximum(base - w, 0), w)

    def scores(r, hh):
        _, start = positions(r)
        kk = k_ref[pl.ds(start, 2 * w), :]
        q = q_ref[r * w:(r + 1) * w, hh * g * d:(hh + 1) * g * d]
        qt = q.astype(F32).T.astype(BF16)
        zeros = jnp.zeros((d, w), BF16)
        cols = [jnp.concatenate([qt[gg * d:(gg + 1) * d], zeros] if hh == 0
                                else [zeros, qt[gg * d:(gg + 1) * d]], axis=0)
                for gg in range(g)]
        return jnp.dot(kk, jnp.concatenate(cols, axis=1), preferred_element_type=F32)

    k_rel = lax.broadcasted_iota(jnp.int32, (2 * w, w), 0)
    q_rel = lax.broadcasted_iota(jnp.int32, (2 * w, w), 1)
    band = (k_rel > q_rel) & (k_rel <= q_rel + w)
    lo, hi = jnp.where(i == 0, -2 * w, 0), jnp.where(i == 0, 0, w)
    band_seq_start = (k_rel > q_rel + lo) & (k_rel <= q_rel + hi)

    def finish(r, hh, s):
        _, start = positions(r)
        mask = band_seq_start if r == 0 else band
        s = jnp.where(jnp.concatenate([mask] * g, axis=1), s, NEG_INF)
        head0 = (p_idx * KV_PAIR + hh) * g
        sink = jnp.concatenate([jnp.full((1, w), sink_ref[head0 + gg] * log2e, F32)
                                for gg in range(g)], axis=1)
        m = jnp.maximum(jnp.max(s, axis=0, keepdims=True), sink)
        e = jnp.exp2(s - m)
        denom = jnp.sum(e, axis=0, keepdims=True) + jnp.exp2(sink - m)
        vv = v_ref[pl.ds(start, 2 * w), :][:, hh * d:(hh + 1) * d]
        ot = lax.dot_general(vv, e.astype(BF16), (((0,), (0,)), ((), ())),
                             preferred_element_type=F32)
        ot = ot * (1.0 / denom)
        o = jnp.concatenate([ot[:, gg * w:(gg + 1) * w] for gg in range(g)], axis=0).T
        o_ref[r * w:(r + 1) * w, hh * g * d:(hh + 1) * g * d] = o.astype(o_ref.dtype)

    s_next = scores(*units[0])
    for n, (r, hh) in enumerate(units):
        s_cur = s_next
        if n + 1 < len(units):
            s_next = scores(*units[n + 1])
        finish(r, hh, s_cur)


def swa_attention(qkv, sinks, batch, seq_len, rows=512):
    t = qkv.shape[0]
    nq = seq_len // rows
    n_pairs = SWA_KV_HEADS // KV_PAIR
    k_blk0 = SWA_Q_WIDTH // V7X_LANES
    v_blk0 = (SWA_Q_WIDTH + SWA_KV_WIDTH) // V7X_LANES
    kern = functools.partial(_swa_kernel, rows=rows)
    return pl.pallas_call(
        kern,
        grid=(batch, n_pairs, nq),
        in_specs=[pl.BlockSpec(memory_space=pltpu.SMEM),
                  pl.BlockSpec((rows, Q_PAIR_WIDTH), lambda b, p, i: (b * nq + i, p)),
                  pl.BlockSpec((seq_len, V7X_LANES), lambda b, p, i: (b, k_blk0 + p)),
                  pl.BlockSpec((seq_len, V7X_LANES), lambda b, p, i: (b, v_blk0 + p))],
        out_specs=pl.BlockSpec((rows, Q_PAIR_WIDTH), lambda b, p, i: (b * nq + i, p)),
        out_shape=jax.ShapeDtypeStruct((t, SWA_Q_WIDTH), BF16),
        compiler_params=_compiler_params(("parallel", "parallel", "arbitrary")),
        name="swa_attention",
    )(sinks.astype(F32), qkv, qkv, qkv)


def _softmax_q_scale(head_dim, q_width, total_width):
    c = head_dim ** -0.5 * math.log2(math.e)
    return jnp.concatenate([jnp.full((q_width,), c, F32),
                            jnp.ones((total_width - q_width,), F32)])


def _lambda_init(layer_idx):
    return 0.8 - 0.6 * math.exp(-0.3 * layer_idx)


def kernel(x, norm_mix, norm_mlp, norm_final, w_in_even, w_pool, pool_scale, lambda_q1, lambda_k1, lambda_q2, lambda_k2, subln_g, w_out_even, w_in_odd, b_in_odd, sinks, w_out_odd, b_out_odd, w_up, w_down):
    b, s, d = x.shape
    x = x.reshape(b * s, d)
    diff_q_scale = _softmax_q_scale(DIFF_HEAD_DIM, DIFF_WIDTH, 3 * DIFF_WIDTH)
    swa_q_scale = _softmax_q_scale(SWA_HEAD_DIM, SWA_Q_WIDTH, SWA_Q_WIDTH + 2 * SWA_KV_WIDTH)
    w_pool = w_pool.astype(BF16)

    def mixer_weights(layer):
        return (w_in_even, w_out_even) if layer % 2 == 0 else (w_in_odd, w_out_odd)

    w_in = mixer_weights(0)[0][0:1].astype(BF16)
    w_out = w_up_b = None
    xg, ssq = rmsnorm(x, norm_mix[0], BF16), None
    for i in range(DEPTH):
        j = i // 2
        if i % 2 == 0:
            u = matmul([xg], w_in, 0, n=POOL_WIDTH, row_ssq=ssq, out_dtype=F32)
            qkv = matmul([xg], w_in, 0, col0=POOL_WIDTH, row_ssq=ssq, col_scale=diff_q_scale,
                         cast_jobs=[(w_out_even, 0), (w_up, 0)] if i == 0 else [])
            if i == 0:
                qkv, w_out, w_up_b = qkv
            o_a = multiscale_pool(u, w_pool[j], pool_scale[j], s)
            o_b = diff_attention(qkv, lambda_q1[j], lambda_k1[j], lambda_q2[j], lambda_k2[j],
                                 subln_g[j], _lambda_init(i), b, s)
            x, xg, ssq = matmul([o_a, o_b], w_out, 0, residual=x, next_gain=norm_mlp[i],
                                out_dtype=F32)
        else:
            qkv = matmul([xg], w_in, 0, row_ssq=ssq, bias=b_in_odd[j], col_scale=swa_q_scale)
            o_c = swa_attention(qkv, sinks[j], b, s)
            x, xg, ssq = matmul([o_c], w_out, 0, bias=b_out_odd[j], residual=x,
                                next_gain=norm_mlp[i], out_dtype=F32)
        jobs = [(w_down, i)]
        if i + 1 < DEPTH:
            nxt_in, nxt_out = mixer_weights(i + 1)
            jobs += [(w_up, i + 1), (nxt_in, (i + 1) // 2), (nxt_out, (i + 1) // 2)]
        a, w_down_b, *nxt = matmul([xg], w_up_b, 0, row_ssq=ssq, relu2=True, cast_jobs=jobs)
        if i + 1 < DEPTH:
            x, xg, ssq = matmul([a], w_down_b, 0, residual=x, next_gain=norm_mix[i + 1],
                                out_dtype=F32, bk=4096)
            w_up_b, w_in, w_out = nxt
        else:
            x = matmul([a], w_down_b, 0, residual=x, out_dtype=F32, bk=4096)
    return rmsnorm(x, norm_final, F32).reshape(b, s, d)
```
